```python
import math
import jax, jax.numpy as jnp
from jax import lax
import numpy as np

D_MODEL = 4096
BATCH = 1
SEQ = 16384
DEPTH = 2

HEAD_DIM = 128
ROT_DIM = HEAD_DIM // 4
ROPE_THETA = 500000.0
MOBA_HEADS = D_MODEL // (2 * HEAD_DIM)
MOBA_WIDTH = MOBA_HEADS * HEAD_DIM
MOBA_BLOCK = 256
MOBA_TOPK = 3
MOBA_QCHUNK = 32
CONV_CH = D_MODEL // 2
CONV_WIDTH = 31
DIL_CONFIGS = ((128, 1), (512, 4), (2048, 16))
DIL_HEADS = D_MODEL // (2 * HEAD_DIM)
DIL_WIDTH = DIL_HEADS * HEAD_DIM
D_FF = 4 * D_MODEL
ALPHA = (2 * DEPTH) ** 0.25
BETA = (8 * DEPTH) ** -0.25
LN_EPS = 1e-5
N_EVEN = (DEPTH + 1) // 2
N_ODD = DEPTH // 2
W_AB_IN = 3 * MOBA_WIDTH + 2 * CONV_CH
W_C_IN = len(DIL_CONFIGS) * 3 * DIL_WIDTH

kernel_name = 'hybrid_moba_conformer_dilated_deepnorm'


def _layernorm(x, g, b):
    xf = x.astype(jnp.float32)
    mu = jnp.mean(xf, axis=-1, keepdims=True)
    var = jnp.mean(jnp.square(xf - mu), axis=-1, keepdims=True)
    y = (xf - mu) * lax.rsqrt(var + LN_EPS) * g.astype(jnp.float32) + b.astype(jnp.float32)
    return y.astype(x.dtype)


def _rope_tables(seq):
    pos = jnp.arange(seq, dtype=jnp.float32)
    inv = ROPE_THETA ** (-jnp.arange(0, ROT_DIM, 2, dtype=jnp.float32) / ROT_DIM)
    ang = pos[:, None] * inv[None, :]
    return jnp.cos(ang), jnp.sin(ang)


def _apply_partial_rope(x, cos, sin):
    half = ROT_DIM // 2
    shape = (x.shape[1],) + (1,) * (x.ndim - 3) + (half,)
    cos = cos.reshape(shape).astype(x.dtype)
    sin = sin.reshape(shape).astype(x.dtype)
    x1 = x[..., :half]
    x2 = x[..., half:ROT_DIM]
    return jnp.concatenate([x1 * cos - x2 * sin, x2 * cos + x1 * sin, x[..., ROT_DIM:]], axis=-1)


def _moba_attention(q, k, v):
    b, h, s, dh = q.shape
    nb = -(-s // MOBA_BLOCK)
    s_pad = nb * MOBA_BLOCK
    pad = ((0, 0), (0, 0), (0, s_pad - s), (0, 0))
    q = jnp.pad(q, pad)
    k = jnp.pad(k, pad)
    v = jnp.pad(v, pad)
    kb = k.reshape(b, h, nb, MOBA_BLOCK, dh)
    vb = v.reshape(b, h, nb, MOBA_BLOCK, dh)
    k_mean = jnp.mean(kb.astype(jnp.float32), axis=3)
    gate = jnp.einsum('bhsd,bhnd->bhsn', q.astype(jnp.float32), k_mean)
    q_blk = jnp.arange(s_pad) // MOBA_BLOCK
    fully_past = jnp.arange(nb)[None, :] < q_blk[:, None]
    gate = jnp.where(fully_past, gate, -jnp.inf)
    n_sel = min(MOBA_TOPK, nb)
    _, sel_idx = lax.top_k(gate, n_sel)
    sel_valid = jnp.arange(n_sel)[None, :] < q_blk[:, None]
    n_chunks = s_pad // MOBA_QCHUNK
    scale = dh ** -0.5
    q_c = q.reshape(b, h, n_chunks, MOBA_QCHUNK, dh).transpose(2, 0, 1, 3, 4)
    idx_c = sel_idx.reshape(b, h, n_chunks, MOBA_QCHUNK, n_sel).transpose(2, 0, 1, 3, 4)
    valid_c = sel_valid.reshape(n_chunks, MOBA_QCHUNK, n_sel)
    b_ix = jnp.arange(b)[:, None, None, None]
    h_ix = jnp.arange(h)[None, :, None, None]

    def chunk(args):
        qc, idx, valid, ci = args
        k_sel = kb[b_ix, h_ix, idx]
        v_sel = vb[b_ix, h_ix, idx]
        s_sel = jnp.einsum('bhqd,bhqjkd->bhqjk', qc, k_sel).astype(jnp.float32) * scale
        s_sel = jnp.where(valid[None, None, :, :, None], s_sel, -jnp.inf)
        blk = (ci * MOBA_QCHUNK) // MOBA_BLOCK
        k_own = lax.dynamic_index_in_dim(kb, blk, axis=2, keepdims=False)
        v_own = lax.dynamic_index_in_dim(vb, blk, axis=2, keepdims=False)
        s_own = jnp.einsum('bhqd,bhkd->bhqk', qc, k_own).astype(jnp.float32) * scale
        q_pos = ci * MOBA_QCHUNK + jnp.arange(MOBA_QCHUNK)
        k_pos = blk * MOBA_BLOCK + jnp.arange(MOBA_BLOCK)
        s_own = jnp.where(k_pos[None, :] <= q_pos[:, None], s_own, -jnp.inf)
        sc = jnp.concatenate([s_sel.reshape(b, h, MOBA_QCHUNK, n_sel * MOBA_BLOCK), s_own], axis=-1)
        p = jax.nn.softmax(sc, axis=-1).astype(v.dtype)
        p_sel = p[..., :n_sel * MOBA_BLOCK].reshape(b, h, MOBA_QCHUNK, n_sel, MOBA_BLOCK)
        p_own = p[..., n_sel * MOBA_BLOCK:]
        return (jnp.einsum('bhqjk,bhqjkd->bhqd', p_sel, v_sel)
                + jnp.einsum('bhqk,bhkd->bhqd', p_own, v_own))

    out = lax.map(chunk, (q_c, idx_c, valid_c, jnp.arange(n_chunks)))
    out = out.transpose(1, 2, 0, 3, 4).reshape(b, h, s_pad, dh)
    return out[:, :, :s]


def _banded_window_attention(q, k, v, window):
    lead = tuple(q.shape[:-2])
    L, dh = q.shape[-2], q.shape[-1]
    blk = window
    nblk = -(-L // blk)
    L_pad = nblk * blk
    nl = len(lead)
    q = jnp.pad(q, [(0, 0)] * nl + [(0, L_pad - L), (0, 0)])
    k = jnp.pad(k, [(0, 0)] * nl + [(blk, L_pad - L), (0, 0)])
    v = jnp.pad(v, [(0, 0)] * nl + [(blk, L_pad - L), (0, 0)])
    qb = q.reshape(lead + (nblk, blk, dh))
    kb = k.reshape(lead + (nblk + 1, blk, dh))
    vb = v.reshape(lead + (nblk + 1, blk, dh))
    k_slab = jnp.concatenate([kb[..., :-1, :, :], kb[..., 1:, :, :]], axis=-2)
    v_slab = jnp.concatenate([vb[..., :-1, :, :], vb[..., 1:, :, :]], axis=-2)
    s = jnp.einsum('...nqd,...nkd->...nqk', qb, k_slab).astype(jnp.float32) * (dh ** -0.5)
    q_pos = (jnp.arange(nblk) * blk)[:, None] + jnp.arange(blk)[None, :]
    k_pos = (jnp.arange(nblk) * blk - blk)[:, None] + jnp.arange(2 * blk)[None, :]
    dist = q_pos[:, :, None] - k_pos[:, None, :]
    mask = (dist >= 0) & (dist <= window) & (k_pos[:, None, :] >= 0)
    s = jnp.where(mask, s, -jnp.inf)
    m = jnp.max(s, axis=-1, keepdims=True)
    p = jnp.exp(s - m)
    l = jnp.sum(p, axis=-1, keepdims=True)
    o = jnp.einsum('...nqk,...nkd->...nqd', p.astype(v.dtype), v_slab).astype(jnp.float32) / l
    lse = (m + jnp.log(l))[..., 0]
    o = o.reshape(lead + (L_pad, dh))[..., :L, :]
    lse = lse.reshape(lead + (L_pad,))[..., :L]
    return o, lse


def _dilated_attention(q, k, v, window, dil):
    b, g, s, dh = q.shape
    L = s // dil

    def split(t):
        return t.reshape(b, g, L, dil, dh).transpose(0, 1, 3, 2, 4)

    o, lse = _banded_window_attention(split(q), split(k), split(v), window // dil)
    o = o.transpose(0, 1, 3, 2, 4).reshape(b, g, s, dh)
    lse = lse.transpose(0, 1, 3, 2).reshape(b, g, s)
    return o, lse


def _moba_conv_mixer(h, w_in, w_out, conv_w, conv_b, conv_ln_g, conv_ln_b, cos, sin):
    b, s, _ = h.shape
    proj = h @ w_in
    qkv = proj[..., :3 * MOBA_WIDTH].reshape(b, s, 3, MOBA_HEADS, HEAD_DIM)
    q = _apply_partial_rope(qkv[:, :, 0], cos, sin).transpose(0, 2, 1, 3)
    k = _apply_partial_rope(qkv[:, :, 1], cos, sin).transpose(0, 2, 1, 3)
    v = qkv[:, :, 2].transpose(0, 2, 1, 3)
    a_out = _moba_attention(q, k, v).transpose(0, 2, 1, 3).reshape(b, s, MOBA_WIDTH)
    glu = proj[..., 3 * MOBA_WIDTH:]
    u = glu[..., :CONV_CH] * jax.nn.sigmoid(glu[..., CONV_CH:])
    u = lax.conv_general_dilated(u, conv_w[:, None, :], window_strides=(1,),
                                 padding=((CONV_WIDTH - 1, 0),),
                                 dimension_numbers=('NWC', 'WIO', 'NWC'),
                                 feature_group_count=CONV_CH) + conv_b
    u = jax.nn.silu(_layernorm(u, conv_ln_g, conv_ln_b))
    return jnp.concatenate([a_out, u], axis=-1) @ w_out


def _dilated_mixer(h, w_in, w_out, cos, sin):
    b, s, _ = h.shape
    n_g = len(DIL_CONFIGS)
    qkv = (h @ w_in).reshape(b, s, n_g, 3, DIL_HEADS, HEAD_DIM)
    q = _apply_partial_rope(qkv[:, :, :, 0], cos, sin)
    k = _apply_partial_rope(qkv[:, :, :, 1], cos, sin)
    v = qkv[:, :, :, 2]
    outs = []
    lses = []
    for gi, (window, dil) in enumerate(DIL_CONFIGS):
        o, lse = _dilated_attention(q[:, :, gi].transpose(0, 2, 1, 3), k[:, :, gi].transpose(0, 2, 1, 3),
                                    v[:, :, gi].transpose(0, 2, 1, 3), window, dil)
        outs.append(o)
        lses.append(lse)
    wts = jax.nn.softmax(jnp.stack(lses), axis=0)
    o = jnp.sum(wts[..., None] * jnp.stack(outs), axis=0)
    o = o.astype(h.dtype).transpose(0, 2, 1, 3).reshape(b, s, DIL_WIDTH)
    return o @ w_out


def _sq_relu_mlp(h, w1, w2):
    return jnp.square(jax.nn.relu(h @ w1)) @ w2


def setup_inputs(seed: int = 0) -> dict:
    key = jax.random.key(seed)
    ks = jax.random.split(key, 16)

    def nrm(k, shape, std):
        return jax.random.normal(k, shape, jnp.float32) * std

    col_ab = jnp.concatenate([jnp.ones((2 * MOBA_WIDTH,), jnp.float32),
                              jnp.full((MOBA_WIDTH,), BETA, jnp.float32),
                              jnp.ones((2 * CONV_CH,), jnp.float32)])
    col_c = jnp.ones((len(DIL_CONFIGS), 3, DIL_WIDTH), jnp.float32).at[:, 2].set(BETA).reshape(-1)
    return {
        'x': jax.random.normal(ks[0], (BATCH, SEQ, D_MODEL), jnp.float32),
        'c': jax.random.normal(ks[1], (BATCH, D_MODEL), jnp.float32),
        'w_ada': nrm(ks[2], (DEPTH, D_MODEL, 6 * D_MODEL), 0.1 * D_MODEL ** -0.5),
        'b_ada': nrm(ks[3], (DEPTH, 6 * D_MODEL), 0.01),
        'w_in_ab': nrm(ks[4], (N_EVEN, D_MODEL, W_AB_IN), D_MODEL ** -0.5) * col_ab,
        'w_out_ab': nrm(ks[5], (N_EVEN, MOBA_WIDTH + CONV_CH, D_MODEL), BETA * (MOBA_WIDTH + CONV_CH) ** -0.5),
        'conv_w': nrm(ks[6], (N_EVEN, CONV_WIDTH, CONV_CH), CONV_WIDTH ** -0.5),
        'conv_b': nrm(ks[7], (N_EVEN, CONV_CH), 0.01),
        'conv_ln_g': 1.0 + nrm(ks[8], (N_EVEN, CONV_CH), 0.01),
        'conv_ln_b': nrm(ks[9], (N_EVEN, CONV_CH), 0.01),
        'w_in_c': nrm(ks[10], (N_ODD, D_MODEL, W_C_IN), D_MODEL ** -0.5) * col_c,
        'w_out_c': nrm(ks[11], (N_ODD, DIL_WIDTH, D_MODEL), BETA * DIL_WIDTH ** -0.5),
        'w_ff1': nrm(ks[12], (DEPTH, D_MODEL, D_FF), BETA * D_MODEL ** -0.5),
        'w_ff2': nrm(ks[13], (DEPTH, D_FF, D_MODEL), BETA * D_FF ** -0.5),
        'ln_g': 1.0 + nrm(ks[14], (DEPTH, 2, D_MODEL), 0.01),
        'ln_b': nrm(ks[15], (DEPTH, 2, D_MODEL), 0.01),
    }


def reference(x, c, w_ada, b_ada, w_in_ab, w_out_ab, conv_w, conv_b, conv_ln_g, conv_ln_b,
              w_in_c, w_out_c, w_ff1, w_ff2, ln_g, ln_b):
    b, s, d = x.shape
    cos, sin = _rope_tables(s)
    cond = jax.nn.silu(c)
    for i in range(DEPTH):
        mod = (cond @ w_ada[i] + b_ada[i]).reshape(b, 6, d)[:, :, None, :]
        shift1, scale1, gate1 = mod[:, 0], mod[:, 1], mod[:, 2]
        shift2, scale2, gate2 = mod[:, 3], mod[:, 4], mod[:, 5]
        h = x * (1.0 + scale1) + shift1
        if i % 2 == 0:
            j = i // 2
            y = _moba_conv_mixer(h, w_in_ab[j], w_out_ab[j], conv_w[j], conv_b[j],
                                 conv_ln_g[j], conv_ln_b[j], cos, sin)
        else:
            j = i // 2
            y = _dilated_mixer(h, w_in_c[j], w_out_c[j], cos, sin)
        x = _layernorm(ALPHA * x + (1.0 + gate1) * y, ln_g[i, 0], ln_b[i, 0])
        h = x * (1.0 + scale2) + shift2
        x = _layernorm(ALPHA * x + (1.0 + gate2) * _sq_relu_mlp(h, w_ff1[i], w_ff2[i]),
                       ln_g[i, 1], ln_b[i, 1])
    return x
```

```python
import functools

import jax
import jax.numpy as jnp
from jax import lax
from jax.experimental import pallas as pl
from jax.experimental.pallas import tpu as pltpu

D_MODEL = 4096
DEPTH = 2
HEAD_DIM = 128
ROT_DIM = HEAD_DIM // 4
ROT_HALF = ROT_DIM // 2
ROPE_THETA = 500000.0
N_HEADS = D_MODEL // (2 * HEAD_DIM)
MIX_WIDTH = N_HEADS * HEAD_DIM
MOBA_BLOCK = 256
MOBA_TOPK = 3
CONV_CH = D_MODEL // 2
CONV_WIDTH = 31
DIL_CONFIGS = ((128, 1), (512, 4), (2048, 16))
DIL_WINDOW = 128
D_FF = 4 * D_MODEL
ALPHA = (2 * DEPTH) ** 0.25
LN_EPS = 1e-5
SM_SCALE = HEAD_DIM ** -0.5

LANES = 128
V7X_VMEM_LIMIT = 56 * 1024 * 1024
NEG = -1e30

_BF16 = jnp.bfloat16
_F32 = jnp.float32


def _params(sem, vmem=None):
    return pltpu.CompilerParams(dimension_semantics=sem, vmem_limit_bytes=vmem)


_ADA_TN = 512
_ADA_ROWS = 512


def _ada_kernel(c_ref, w_ref, b_ref, o_ref, cond_ref):
    @pl.when((pl.program_id(0) == 0) & (pl.program_id(1) == 0))
    def _():
        cv = c_ref[...]
        cond_ref[...] = cv * jax.nn.sigmoid(cv)

    d = w_ref.shape[0]
    acc = jnp.zeros((8, _ADA_TN), _F32)
    for r0 in range(0, d, _ADA_ROWS):
        cond = cond_ref[r0:r0 + _ADA_ROWS, :]
        w = w_ref[r0:r0 + _ADA_ROWS, :]
        cond_t = jnp.concatenate([cond] * (_ADA_TN // LANES), axis=1)
        prod = (w * cond_t).reshape(_ADA_ROWS // 8, 8, _ADA_TN)
        acc = acc + jnp.sum(prod, axis=0)
    o_ref[...] = jnp.sum(acc, axis=0, keepdims=True) + b_ref[...]


def _ada_modulation(c, w_ada, b_ada):
    depth, d, n = w_ada.shape
    c_rep = jnp.broadcast_to(c.reshape(d, 1), (d, LANES))
    out = pl.pallas_call(
        _ada_kernel,
        out_shape=jax.ShapeDtypeStruct((depth, 1, n), _F32),
        grid=(depth, n // _ADA_TN),
        in_specs=[
            pl.BlockSpec((d, LANES), lambda l, j: (0, 0)),
            pl.BlockSpec((None, d, _ADA_TN), lambda l, j: (l, 0, j)),
            pl.BlockSpec((None, 1, _ADA_TN), lambda l, j: (l, 0, j)),
        ],
        out_specs=pl.BlockSpec((None, 1, _ADA_TN), lambda l, j: (l, 0, j)),
        scratch_shapes=[pltpu.VMEM((d, LANES), _F32)],
        compiler_params=_params(("arbitrary", "arbitrary"), 40 * 1024 * 1024),
        name="ada_modulation",
    )(c_rep, w_ada, b_ada.reshape(depth, 1, n))
    return out.reshape(depth * 6, 1, d)


_ROW_TM = 256


def _modulate_kernel(x_ref, shift_ref, scale_ref, h_ref):
    h_ref[...] = (x_ref[...] * (1.0 + scale_ref[...]) + shift_ref[...]).astype(h_ref.dtype)


def _modulate(x, mod, shift_row, scale_row):
    s, d = x.shape
    return pl.pallas_call(
        _modulate_kernel,
        out_shape=jax.ShapeDtypeStruct((s, d), _BF16),
        grid=(s // _ROW_TM,),
        in_specs=[
            pl.BlockSpec((_ROW_TM, d), lambda i: (i, 0)),
            pl.BlockSpec((None, 1, d), lambda i: (shift_row, 0, 0)),
            pl.BlockSpec((None, 1, d), lambda i: (scale_row, 0, 0)),
        ],
        out_specs=pl.BlockSpec((_ROW_TM, d), lambda i: (i, 0)),
        compiler_params=_params(("parallel",)),
        name="modulate",
    )(x, mod, mod)


def _ln_kernel(x_ref, y_ref, gate_ref, g_ref, b_ref, *rest, with_h):
    if with_h:
        shift_ref, scale_ref, xo_ref, h_ref = rest
    else:
        (xo_ref,) = rest
    z = ALPHA * x_ref[...] + (1.0 + gate_ref[...]) * y_ref[...]
    mu = jnp.mean(z, axis=-1, keepdims=True)
    zc = z - mu
    var = jnp.mean(zc * zc, axis=-1, keepdims=True)
    xn = zc * lax.rsqrt(var + LN_EPS) * g_ref[...] + b_ref[...]
    xo_ref[...] = xn
    if with_h:
        h_ref[...] = (xn * (1.0 + scale_ref[...]) + shift_ref[...]).astype(h_ref.dtype)


def _residual_ln(x, y, mod, gate_row, ln_g, ln_b, next_rows=None):
    s, d = x.shape
    with_h = next_rows is not None
    row = lambda r: pl.BlockSpec((None, 1, d), lambda i: (r, 0, 0))
    vec = pl.BlockSpec((1, d), lambda i: (0, 0))
    tile = pl.BlockSpec((_ROW_TM, d), lambda i: (i, 0))
    in_specs = [tile, tile, row(gate_row), vec, vec]
    args = [x, y, mod, ln_g.reshape(1, d), ln_b.reshape(1, d)]
    out_shape = [jax.ShapeDtypeStruct((s, d), _F32)]
    out_specs = [tile]
    if with_h:
        in_specs += [row(next_rows[0]), row(next_rows[1])]
        args += [mod, mod]
        out_shape.append(jax.ShapeDtypeStruct((s, d), _BF16))
        out_specs.append(tile)
    res = pl.pallas_call(
        functools.partial(_ln_kernel, with_h=with_h),
        out_shape=out_shape,
        grid=(s // _ROW_TM,),
        in_specs=in_specs,
        out_specs=out_specs,
        compiler_params=_params(("parallel",), 40 * 1024 * 1024),
        name="residual_ln",
    )(*args)
    return (res[0], res[1]) if with_h else (res[0], None)


_MM_TM = 1024
_MM_TN = 1024
_MM_TK = 2048


def _rope_tile(y, c, sa, sb):
    return y * c + pltpu.roll(y, LANES - ROT_HALF, 1) * sa + pltpu.roll(y, ROT_HALF, 1) * sb


def _mm_qkv_kernel(a_ref, b_ref, c_ref, sa_ref, sb_ref, o_ref):
    tn = o_ref.shape[1]
    acc = jnp.dot(a_ref[...], b_ref[...], preferred_element_type=_F32)
    which = ((pl.program_id(1) * tn) // MIX_WIDTH) % 3

    @pl.when(which == 2)
    def _():
        o_ref[...] = acc.astype(o_ref.dtype)

    @pl.when(which != 2)
    def _():
        qs = jnp.where(which == 0, SM_SCALE, 1.0).astype(_F32)
        c = c_ref[...] * qs
        sa = sa_ref[...] * qs
        sb = sb_ref[...] * qs
        for h0 in range(0, tn, HEAD_DIM):
            y = _rope_tile(acc[:, h0:h0 + HEAD_DIM], c, sa, sb)
            o_ref[:, h0:h0 + HEAD_DIM] = y.astype(o_ref.dtype)


def _matmul_qkv(a, b, rope_c, rope_sa, rope_sb):
    m, k = a.shape
    n = b.shape[1]
    tm, tn = min(_MM_TM, m), _MM_TN
    tab = pl.BlockSpec((tm, LANES), lambda i, j: (i, 0))
    return pl.pallas_call(
        _mm_qkv_kernel,
        out_shape=jax.ShapeDtypeStruct((m, n), _BF16),
        grid=(m // tm, n // tn),
        in_specs=[
            pl.BlockSpec((tm, k), lambda i, j: (i, 0)),
            pl.BlockSpec((k, tn), lambda i, j: (0, j)),
            tab, tab, tab,
        ],
        out_specs=pl.BlockSpec((tm, tn), lambda i, j: (i, j)),
        compiler_params=_params(("parallel", "arbitrary"), V7X_VMEM_LIMIT),
        name="matmul_qkv_rope",
    )(a, b, rope_c, rope_sa, rope_sb)


def _mm_kernel(a_ref, b_ref, o_ref, *, act):
    acc = jnp.dot(a_ref[...], b_ref[...], preferred_element_type=_F32)
    if act == "sqrelu":
        acc = jnp.square(jnp.maximum(acc, 0.0))
    o_ref[...] = acc.astype(o_ref.dtype)


def _mm_acc_kernel(a_ref, b_ref, o_ref, acc_ref):
    kk = pl.program_id(2)

    @pl.when(kk == 0)
    def _():
        acc_ref[...] = jnp.zeros_like(acc_ref)

    acc_ref[...] += jnp.dot(a_ref[...], b_ref[...], preferred_element_type=_F32)

    @pl.when(kk == pl.num_programs(2) - 1)
    def _():
        o_ref[...] = acc_ref[...].astype(o_ref.dtype)


def _matmul(a, b, out_dtype, act=None):
    m, k = a.shape
    n = b.shape[1]
    tm, tn = min(_MM_TM, m), _MM_TN
    if k <= 4096:
        return pl.pallas_call(
            functools.partial(_mm_kernel, act=act),
            out_shape=jax.ShapeDtypeStruct((m, n), out_dtype),
            grid=(m // tm, n // tn),
            in_specs=[
                pl.BlockSpec((tm, k), lambda i, j: (i, 0)),
                pl.BlockSpec((k, tn), lambda i, j: (0, j)),
            ],
            out_specs=pl.BlockSpec((tm, tn), lambda i, j: (i, j)),
            compiler_params=_params(("parallel", "arbitrary"), V7X_VMEM_LIMIT),
            name="matmul",
        )(a, b)
    assert act is None
    tk = _MM_TK
    return pl.pallas_call(
        _mm_acc_kernel,
        out_shape=jax.ShapeDtypeStruct((m, n), out_dtype),
        grid=(m // tm, n // tn, k // tk),
        in_specs=[
            pl.BlockSpec((tm, tk), lambda i, j, kk: (i, kk)),
            pl.BlockSpec((tk, tn), lambda i, j, kk: (kk, j)),
        ],
        out_specs=pl.BlockSpec((tm, tn), lambda i, j, kk: (i, j)),
        scratch_shapes=[pltpu.VMEM((tm, tn), _F32)],
        compiler_params=_params(("parallel", "arbitrary", "arbitrary"), V7X_VMEM_LIMIT),
        name="matmul_kacc",
    )(a, b)


def _dot_nt(a, b):
    return lax.dot_general(a, b, (((1,), (1,)), ((), ())), preferred_element_type=_F32)


def _moba_kernel(q_ref, k_ref, v_ref, o_ref, kmh_ref, kml_ref):
    i = pl.program_id(1)
    blk = MOBA_BLOCK
    nb = k_ref.shape[0] // blk
    nbp = kmh_ref.shape[0]

    @pl.when(i == 0)
    def _():
        kmh_ref[...] = jnp.zeros_like(kmh_ref)
        kml_ref[...] = jnp.zeros_like(kml_ref)

        def body(b, carry):
            start = pl.multiple_of(b * blk, blk)
            kb = k_ref[pl.ds(start, blk), :].astype(_F32)
            mean = jnp.sum(kb, axis=0, keepdims=True) * (1.0 / blk)
            hi = mean.astype(_BF16)
            kmh_ref[pl.ds(b, 1), :] = hi.astype(_F32)
            kml_ref[pl.ds(b, 1), :] = mean - hi.astype(_F32)
            return carry

        lax.fori_loop(0, nb, body, 0)

    q = q_ref[...]
    gate = _dot_nt(q, kmh_ref[...].astype(_BF16)) + _dot_nt(q, kml_ref[...].astype(_BF16))
    col = lax.broadcasted_iota(jnp.int32, (blk, nbp), 1)
    g = jnp.where(col < i, gate, -jnp.inf)
    sel = jnp.zeros((blk, nbp), jnp.bool_)
    for r in range(MOBA_TOPK):
        mx = jnp.max(g, axis=-1, keepdims=True)
        first = jnp.min(jnp.where(g == mx, col, nbp), axis=-1, keepdims=True)
        pick = (col == first) & (r < i)
        sel = sel | pick
        g = jnp.where(pick, -jnp.inf, g)
    selbias = jnp.where(sel, 0.0, NEG).astype(_F32)

    def step(s, vj, m, l, acc):
        m_new = jnp.maximum(m, jnp.max(s, axis=-1, keepdims=True))
        alpha = jnp.exp(m - m_new)
        p = jnp.exp(s - m_new)
        l = alpha * l + jnp.sum(p, axis=-1, keepdims=True)
        acc = alpha * acc + jnp.dot(p.astype(_BF16), vj, preferred_element_type=_F32)
        return m_new, l, acc

    def body(j, carry):
        m, l, acc = carry
        start = pl.multiple_of(j * blk, blk)
        kj = k_ref[pl.ds(start, blk), :]
        vj = v_ref[pl.ds(start, blk), :]
        rowbias = jnp.max(jnp.where(col == j, selbias, NEG), axis=-1, keepdims=True)
        s = jnp.maximum(_dot_nt(q, kj) + rowbias, NEG)
        return step(s, vj, m, l, acc)

    init = (jnp.full((blk, 1), NEG, _F32), jnp.zeros((blk, 1), _F32), jnp.zeros((blk, HEAD_DIM), _F32))
    m, l, acc = lax.fori_loop(0, i, body, init)

    start = pl.multiple_of(i * blk, blk)
    kj = k_ref[pl.ds(start, blk), :]
    vj = v_ref[pl.ds(start, blk), :]
    qpos = lax.broadcasted_iota(jnp.int32, (blk, blk), 0)
    kpos = lax.broadcasted_iota(jnp.int32, (blk, blk), 1)
    s = jnp.where(kpos <= qpos, _dot_nt(q, kj), NEG)
    m, l, acc = step(s, vj, m, l, acc)
    o_ref[...] = (acc / l).astype(o_ref.dtype)


def _moba_attention(qkv):
    s = qkv.shape[0]
    nb = s // MOBA_BLOCK
    nbp = max(LANES // 2, -(-nb // 8) * 8)
    return pl.pallas_call(
        _moba_kernel,
        out_shape=jax.ShapeDtypeStruct((s, MIX_WIDTH), _BF16),
        grid=(N_HEADS, nb),
        in_specs=[
            pl.BlockSpec((MOBA_BLOCK, HEAD_DIM), lambda h, i: (i, h)),
            pl.BlockSpec((s, HEAD_DIM), lambda h, i: (0, N_HEADS + h)),
            pl.BlockSpec((s, HEAD_DIM), lambda h, i: (0, 2 * N_HEADS + h)),
        ],
        out_specs=pl.BlockSpec((MOBA_BLOCK, HEAD_DIM), lambda h, i: (i, h)),
        scratch_shapes=[pltpu.VMEM((nbp, HEAD_DIM), _F32), pltpu.VMEM((nbp, HEAD_DIM), _F32)],
        compiler_params=_params(("parallel", "arbitrary"), 40 * 1024 * 1024),
        name="moba_attention",
    )(qkv, qkv, qkv)


_CONV_TR = 128
_CONV_HALO = 32
_CONV_LC = 256


def _conv_kernel(cur_ref, halo_ref, w_ref, cb_ref, g_ref, b_ref, o_ref, u_ref, y_ref):
    i = pl.program_id(0)
    ch = CONV_CH
    cur = cur_ref[...]
    u_ref[_CONV_HALO:, :] = cur[:, :ch] * jax.nn.sigmoid(cur[:, ch:])
    halo = halo_ref[...]
    uh = halo[:, :ch] * jax.nn.sigmoid(halo[:, ch:])
    u_ref[:_CONV_HALO, :] = jnp.where(i > 0, uh, 0.0)

    off = _CONV_HALO - (CONV_WIDTH - 1)
    for c0 in range(0, ch, _CONV_LC):
        acc = jnp.zeros((_CONV_TR, _CONV_LC), _F32) + cb_ref[:, c0:c0 + _CONV_LC]
        for t in range(CONV_WIDTH):
            acc = acc + u_ref[off + t:off + t + _CONV_TR, c0:c0 + _CONV_LC] * w_ref[t:t + 1, c0:c0 + _CONV_LC]
        y_ref[:, c0:c0 + _CONV_LC] = acc

    y = y_ref[...]
    mu = jnp.mean(y, axis=-1, keepdims=True)
    yc = y - mu
    var = jnp.mean(yc * yc, axis=-1, keepdims=True)
    yn = yc * lax.rsqrt(var + LN_EPS) * g_ref[...] + b_ref[...]
    o_ref[...] = (yn * jax.nn.sigmoid(yn)).astype(o_ref.dtype)


def _conv_module(glu, conv_w, conv_b, ln_g, ln_b):
    s = glu.shape[0]
    ch = CONV_CH
    ratio = _CONV_TR // _CONV_HALO
    vec = lambda: pl.BlockSpec((1, ch), lambda i: (0, 0))
    return pl.pallas_call(
        _conv_kernel,
        out_shape=jax.ShapeDtypeStruct((s, ch), _BF16),
        grid=(s // _CONV_TR,),
        in_specs=[
            pl.BlockSpec((_CONV_TR, 2 * ch), lambda i: (i, 0)),
            pl.BlockSpec((_CONV_HALO, 2 * ch), lambda i: (jnp.maximum(i * ratio - 1, 0), 0)),
            pl.BlockSpec((CONV_WIDTH, ch), lambda i: (0, 0)),
            vec(), vec(), vec(),
        ],
        out_specs=pl.BlockSpec((_CONV_TR, ch), lambda i: (i, 0)),
        scratch_shapes=[pltpu.VMEM((_CONV_TR + _CONV_HALO, ch), _F32), pltpu.VMEM((_CONV_TR, ch), _F32)],
        compiler_params=_params(("parallel",), 40 * 1024 * 1024),
        name="conformer_conv",
    )(glu, glu, conv_w, conv_b.reshape(1, ch), ln_g.reshape(1, ch), ln_b.reshape(1, ch))


def _dil_kernel(q_ref, kp_ref, kc_ref, vp_ref, vc_ref, o_ref, lse_ref):
    n = pl.program_id(1)
    w = DIL_WINDOW
    a = lax.broadcasted_iota(jnp.int32, (w, w), 0)
    b = lax.broadcasted_iota(jnp.int32, (w, w), 1)
    mask_prev = (b >= a) & (n > 0)
    mask_cur = b <= a
    lane = lax.broadcasted_iota(jnp.int32, (w, LANES), 1)
    lse_tile = jnp.zeros((w, LANES), _F32)
    for hh in range(N_HEADS):
        sl = slice(hh * HEAD_DIM, (hh + 1) * HEAD_DIM)
        q = q_ref[:, sl]
        sp = jnp.where(mask_prev, _dot_nt(q, kp_ref[:, sl]), NEG)
        sc = jnp.where(mask_cur, _dot_nt(q, kc_ref[:, sl]), NEG)
        m = jnp.maximum(jnp.max(sp, axis=-1, keepdims=True), jnp.max(sc, axis=-1, keepdims=True))
        pp = jnp.exp(sp - m)
        pc = jnp.exp(sc - m)
        l = jnp.sum(pp, axis=-1, keepdims=True) + jnp.sum(pc, axis=-1, keepdims=True)
        o = (jnp.dot(pp.astype(_BF16), vp_ref[:, sl], preferred_element_type=_F32)
             + jnp.dot(pc.astype(_BF16), vc_ref[:, sl], preferred_element_type=_F32))
        o_ref[:, sl] = o / l
        lse_tile = jnp.where(lane == hh, m + jnp.log(l), lse_tile)
    lse_ref[...] = lse_tile


def _dilated_group(qkv, group, dil):
    s, c = qkv.shape
    w = DIL_WINDOW
    ls = s // dil
    nblk = ls // w
    ncol = c // MIX_WIDTH
    view = qkv.reshape(ls, dil * c)
    base = 3 * group

    def spec(slot, prev):
        if prev:
            return pl.BlockSpec((w, MIX_WIDTH), lambda r, n: (jnp.maximum(n - 1, 0), r * ncol + base + slot))
        return pl.BlockSpec((w, MIX_WIDTH), lambda r, n: (n, r * ncol + base + slot))

    o, lse = pl.pallas_call(
        _dil_kernel,
        out_shape=[jax.ShapeDtypeStruct((ls, dil * MIX_WIDTH), _F32),
                   jax.ShapeDtypeStruct((ls, dil * LANES), _F32)],
        grid=(dil, nblk),
        in_specs=[spec(0, False), spec(1, True), spec(1, False), spec(2, True), spec(2, False)],
        out_specs=[pl.BlockSpec((w, MIX_WIDTH), lambda r, n: (n, r)),
                   pl.BlockSpec((w, LANES), lambda r, n: (n, r))],
        compiler_params=_params(("parallel", "arbitrary")),
        name=f"dilated_attention_d{dil}",
    )(view, view, view, view, view)
    return o.reshape(s, MIX_WIDTH), lse.reshape(s, LANES)


def _merge_kernel(o0_ref, o1_ref, o2_ref, l0_ref, l1_ref, l2_ref, out_ref):
    l0, l1, l2 = l0_ref[...], l1_ref[...], l2_ref[...]
    mx = jnp.maximum(jnp.maximum(l0, l1), l2)
    e0, e1, e2 = jnp.exp(l0 - mx), jnp.exp(l1 - mx), jnp.exp(l2 - mx)
    inv = 1.0 / (e0 + e1 + e2)
    w0, w1, w2 = e0 * inv, e1 * inv, e2 * inv
    for hh in range(N_HEADS):
        sl = slice(hh * HEAD_DIM, (hh + 1) * HEAD_DIM)
        o = (w0[:, hh:hh + 1] * o0_ref[:, sl] + w1[:, hh:hh + 1] * o1_ref[:, sl]
             + w2[:, hh:hh + 1] * o2_ref[:, sl])
        out_ref[:, sl] = o.astype(out_ref.dtype)


def _merge_groups(outs, lses):
    s = outs[0].shape[0]
    tm = _ROW_TM
    ot = pl.BlockSpec((tm, MIX_WIDTH), lambda i: (i, 0))
    lt = pl.BlockSpec((tm, LANES), lambda i: (i, 0))
    return pl.pallas_call(
        _merge_kernel,
        out_shape=jax.ShapeDtypeStruct((s, MIX_WIDTH), _BF16),
        grid=(s // tm,),
        in_specs=[ot, ot, ot, lt, lt, lt],
        out_specs=ot,
        compiler_params=_params(("parallel",)),
        name="merge_dilated_groups",
    )(*outs, *lses)


def _rope_tables(s):
    pos = jnp.arange(s, dtype=_F32)
    inv = ROPE_THETA ** (-jnp.arange(0, ROT_DIM, 2, dtype=_F32) / ROT_DIM)
    ang = pos[:, None] * inv[None, :]
    cos, sin = jnp.cos(ang), jnp.sin(ang)
    zeros = jnp.zeros((s, HEAD_DIM - ROT_DIM), _F32)
    zh = jnp.zeros((s, ROT_HALF), _F32)
    c = jnp.concatenate([cos, cos, jnp.ones((s, HEAD_DIM - ROT_DIM), _F32)], axis=1)
    sa = jnp.concatenate([-sin, zh, zeros], axis=1)
    sb = jnp.concatenate([zh, sin, zeros], axis=1)
    return c, sa, sb


def _forward(x, c, w_ada, b_ada, w_in_ab, w_out_ab, conv_w, conv_b, conv_ln_g, conv_ln_b,
             w_in_c, w_out_c, w_ff1, w_ff2, ln_g, ln_b):
    s, d = x.shape
    rope = _rope_tables(s)
    mod = _ada_modulation(c.reshape(d), w_ada, b_ada)
    bf = lambda t: t.astype(_BF16)

    h = _modulate(x, mod, 0, 1)
    w_in = w_in_ab[0]
    qkv = _matmul_qkv(h, bf(w_in[:, :3 * MIX_WIDTH]), *rope)
    glu = _matmul(h, bf(w_in[:, 3 * MIX_WIDTH:]), _F32)
    a_out = _moba_attention(qkv)
    u = _conv_module(glu, conv_w[0], conv_b[0], conv_ln_g[0], conv_ln_b[0])
    y = _matmul(jnp.concatenate([a_out, u], axis=1), bf(w_out_ab[0]), _F32)
    x, h = _residual_ln(x, y, mod, 2, ln_g[0, 0], ln_b[0, 0], (3, 4))
    y = _matmul(_matmul(h, bf(w_ff1[0]), _BF16, act="sqrelu"), bf(w_ff2[0]), _F32)
    x, h = _residual_ln(x, y, mod, 5, ln_g[0, 1], ln_b[0, 1], (6, 7))

    qkv = _matmul_qkv(h, bf(w_in_c[0]), *rope)
    outs, lses = [], []
    for gi, (_, dil) in enumerate(DIL_CONFIGS):
        o, lse = _dilated_group(qkv, gi, dil)
        outs.append(o)
        lses.append(lse)
    y = _matmul(_merge_groups(outs, lses), bf(w_out_c[0]), _F32)
    x, h = _residual_ln(x, y, mod, 8, ln_g[1, 0], ln_b[1, 0], (9, 10))
    y = _matmul(_matmul(h, bf(w_ff1[1]), _BF16, act="sqrelu"), bf(w_ff2[1]), _F32)
    x, _ = _residual_ln(x, y, mod, 11, ln_g[1, 1], ln_b[1, 1])
    return x


def kernel(x, c, w_ada, b_ada, w_in_ab, w_out_ab, conv_w, conv_b, conv_ln_g, conv_ln_b,
           w_in_c, w_out_c, w_ff1, w_ff2, ln_g, ln_b):
    b, s, d = x.shape
    assert b == 1 and d == D_MODEL and s % (DIL_CONFIGS[-1][1] * DIL_WINDOW) == 0
    out = _forward(x.reshape(s, d), c, w_ada, b_ada, w_in_ab, w_out_ab, conv_w, conv_b,
                   conv_ln_g, conv_ln_b, w_in_c, w_out_c, w_ff1, w_ff2, ln_g, ln_b)
    return out.reshape(b, s, d)
```

```python
import functools

import jax
import jax.numpy as jnp
from jax import lax
from jax.experimental import pallas as pl
from jax.experimental.pallas import tpu as pltpu

D_MODEL = 4096
DEPTH = 2
HEAD_DIM = 128
ROT_DIM = HEAD_DIM // 4
ROT_HALF = ROT_DIM // 2
ROPE_THETA = 500000.0
N_HEADS = D_MODEL // (2 * HEAD_DIM)
MIX_WIDTH = N_HEADS * HEAD_DIM
MOBA_BLOCK = 256
MOBA_TOPK = 3
CONV_CH = D_MODEL // 2
CONV_WIDTH = 31
DIL_CONFIGS = ((128, 1), (512, 4), (2048, 16))
DIL_WINDOW = 128
D_FF = 4 * D_MODEL
ALPHA = (2 * DEPTH) ** 0.25
LN_EPS = 1e-5
SM_SCALE = HEAD_DIM ** -0.5
LOG2E = 1.4426950408889634
Q_SCALE = SM_SCALE * LOG2E
STREAMS = DIL_CONFIGS[-1][1]

LANES = 128
V7X_VMEM_LIMIT = 56 * 1024 * 1024
NEG = -1e30

_BF16 = jnp.bfloat16
_F32 = jnp.float32


def _params(sem, vmem=None):
    return pltpu.CompilerParams(dimension_semantics=sem, vmem_limit_bytes=vmem)


_ADA_TN = 512
_ADA_ROWS = 512


def _ada_kernel(c_ref, w_ref, b_ref, o_ref, cond_ref):
    @pl.when((pl.program_id(0) == 0) & (pl.program_id(1) == 0))
    def _():
        cv = c_ref[...]
        cond_ref[...] = cv * jax.nn.sigmoid(cv)

    d = w_ref.shape[0]
    acc = jnp.zeros((8, _ADA_TN), _F32)
    for r0 in range(0, d, _ADA_ROWS):
        cond = cond_ref[r0:r0 + _ADA_ROWS, :]
        w = w_ref[r0:r0 + _ADA_ROWS, :]
        cond_t = jnp.concatenate([cond] * (_ADA_TN // LANES), axis=1)
        prod = (w * cond_t).reshape(_ADA_ROWS // 8, 8, _ADA_TN)
        acc = acc + jnp.sum(prod, axis=0)
    o_ref[...] = jnp.sum(acc, axis=0, keepdims=True) + b_ref[...]


def _ada_modulation(c, w_ada, b_ada):
    depth, d, n = w_ada.shape
    c_rep = jnp.broadcast_to(c.reshape(d, 1), (d, LANES))
    out = pl.pallas_call(
        _ada_kernel,
        out_shape=jax.ShapeDtypeStruct((depth, 1, n), _F32),
        grid=(depth, n // _ADA_TN),
        in_specs=[
            pl.BlockSpec((d, LANES), lambda l, j: (0, 0)),
            pl.BlockSpec((None, d, _ADA_TN), lambda l, j: (l, 0, j)),
            pl.BlockSpec((None, 1, _ADA_TN), lambda l, j: (l, 0, j)),
        ],
        out_specs=pl.BlockSpec((None, 1, _ADA_TN), lambda l, j: (l, 0, j)),
        scratch_shapes=[pltpu.VMEM((d, LANES), _F32)],
        compiler_params=_params(("arbitrary", "arbitrary"), 40 * 1024 * 1024),
        name="ada_modulation",
    )(c_rep, w_ada, b_ada.reshape(depth, 1, n))
    return out.reshape(depth * 6, 1, d)


_ROW_TM = 256


def _modulate_kernel(x_ref, shift_ref, scale_ref, h_ref):
    h_ref[...] = (x_ref[...] * (1.0 + scale_ref[...]) + shift_ref[...]).astype(h_ref.dtype)


def _modulate(x, mod, shift_row, scale_row):
    s, d = x.shape
    return pl.pallas_call(
        _modulate_kernel,
        out_shape=jax.ShapeDtypeStruct((s, d), _BF16),
        grid=(s // _ROW_TM,),
        in_specs=[
            pl.BlockSpec((_ROW_TM, d), lambda i: (i, 0)),
            pl.BlockSpec((None, 1, d), lambda i: (shift_row, 0, 0)),
            pl.BlockSpec((None, 1, d), lambda i: (scale_row, 0, 0)),
        ],
        out_specs=pl.BlockSpec((_ROW_TM, d), lambda i: (i, 0)),
        compiler_params=_params(("parallel",)),
        name="modulate",
    )(x, mod, mod)


_STREAM_ROWS = _ROW_TM // STREAMS


def _stream_permutation():
    assert _STREAM_ROWS == STREAMS
    dst = lax.broadcasted_iota(jnp.int32, (_ROW_TM, _ROW_TM), 0)
    src = lax.broadcasted_iota(jnp.int32, (_ROW_TM, _ROW_TM), 1)
    return (src == (dst % STREAMS) * STREAMS + dst // STREAMS).astype(_BF16)


def _ln_kernel(x_ref, y_ref, gate_ref, g_ref, b_ref, *rest, with_h, stream_h):
    if with_h:
        shift_ref, scale_ref, xo_ref, h_ref = rest
    else:
        (xo_ref,) = rest
    z = ALPHA * x_ref[...] + (1.0 + gate_ref[...]) * y_ref[...]
    mu = jnp.mean(z, axis=-1, keepdims=True)
    zc = z - mu
    var = jnp.mean(zc * zc, axis=-1, keepdims=True)
    xn = zc * lax.rsqrt(var + LN_EPS) * g_ref[...] + b_ref[...]
    xo_ref[...] = xn
    if with_h:
        h = (xn * (1.0 + scale_ref[...]) + shift_ref[...]).astype(h_ref.dtype)
        if stream_h:
            h = jnp.dot(_stream_permutation(), h, preferred_element_type=_F32).astype(h_ref.dtype)
            h = h.reshape(h_ref.shape)
        h_ref[...] = h


def _residual_ln(x, y, mod, gate_row, ln_g, ln_b, next_rows=None, stream_h=False):
    s, d = x.shape
    with_h = next_rows is not None
    row = lambda r: pl.BlockSpec((None, 1, d), lambda i: (r, 0, 0))
    vec = pl.BlockSpec((1, d), lambda i: (0, 0))
    tile = pl.BlockSpec((_ROW_TM, d), lambda i: (i, 0))
    in_specs = [tile, tile, row(gate_row), vec, vec]
    args = [x, y, mod, ln_g.reshape(1, d), ln_b.reshape(1, d)]
    out_shape = [jax.ShapeDtypeStruct((s, d), _F32)]
    out_specs = [tile]
    if with_h:
        in_specs += [row(next_rows[0]), row(next_rows[1])]
        args += [mod, mod]
        if stream_h:
            out_shape.append(jax.ShapeDtypeStruct((STREAMS, s // STREAMS, d), _BF16))
            out_specs.append(pl.BlockSpec((STREAMS, _STREAM_ROWS, d), lambda i: (0, i, 0)))
        else:
            out_shape.append(jax.ShapeDtypeStruct((s, d), _BF16))
            out_specs.append(tile)
    res = pl.pallas_call(
        functools.partial(_ln_kernel, with_h=with_h, stream_h=stream_h),
        out_shape=out_shape,
        grid=(s // _ROW_TM,),
        in_specs=in_specs,
        out_specs=out_specs,
        compiler_params=_params(("parallel",), 40 * 1024 * 1024),
        name="residual_ln",
    )(*args)
    return (res[0], res[1].reshape(s, d)) if with_h else (res[0], None)


_MM_TM = 1024
_MM_TN = 1024
_MM_TK = 2048


def _rope_tile(y, c, sa, sb):
    return y * c + pltpu.roll(y, LANES - ROT_HALF, 1) * sa + pltpu.roll(y, ROT_HALF, 1) * sb


def _mm_qkv_kernel(a_ref, b_ref, c_ref, sa_ref, sb_ref, o_ref):
    tn = o_ref.shape[1]
    acc = jnp.dot(a_ref[...], b_ref[...], preferred_element_type=_F32)
    which = ((pl.program_id(1) * tn) // MIX_WIDTH) % 3

    @pl.when(which == 2)
    def _():
        o_ref[...] = acc.astype(o_ref.dtype)

    @pl.when(which != 2)
    def _():
        qs = jnp.where(which == 0, Q_SCALE, 1.0).astype(_F32)
        c = c_ref[...] * qs
        sa = sa_ref[...] * qs
        sb = sb_ref[...] * qs
        for h0 in range(0, tn, HEAD_DIM):
            y = _rope_tile(acc[:, h0:h0 + HEAD_DIM], c, sa, sb)
            o_ref[:, h0:h0 + HEAD_DIM] = y.astype(o_ref.dtype)


def _matmul_qkv(a, b, rope_c, rope_sa, rope_sb):
    m, k = a.shape
    n = b.shape[1]
    tm, tn = min(_MM_TM, m), _MM_TN
    tab = pl.BlockSpec((tm, LANES), lambda i, j: (i, 0))
    return pl.pallas_call(
        _mm_qkv_kernel,
        out_shape=jax.ShapeDtypeStruct((m, n), _BF16),
        grid=(m // tm, n // tn),
        in_specs=[
            pl.BlockSpec((tm, k), lambda i, j: (i, 0)),
            pl.BlockSpec((k, tn), lambda i, j: (0, j)),
            tab, tab, tab,
        ],
        out_specs=pl.BlockSpec((tm, tn), lambda i, j: (i, j)),
        compiler_params=_params(("parallel", "arbitrary"), V7X_VMEM_LIMIT),
        name="matmul_qkv_rope",
    )(a, b, rope_c, rope_sa, rope_sb)


def _mm_kernel(a_ref, b_ref, o_ref, *, act):
    acc = jnp.dot(a_ref[...], b_ref[...], preferred_element_type=_F32)
    if act == "sqrelu":
        acc = jnp.square(jnp.maximum(acc, 0.0))
    o_ref[...] = acc.astype(o_ref.dtype)


def _mm_acc_kernel(a_ref, b_ref, o_ref, acc_ref):
    kk = pl.program_id(2)

    @pl.when(kk == 0)
    def _():
        acc_ref[...] = jnp.zeros_like(acc_ref)

    acc_ref[...] += jnp.dot(a_ref[...], b_ref[...], preferred_element_type=_F32)

    @pl.when(kk == pl.num_programs(2) - 1)
    def _():
        o_ref[...] = acc_ref[...].astype(o_ref.dtype)


def _matmul(a, b, out_dtype, act=None):
    m, k = a.shape
    n = b.shape[1]
    tm, tn = min(_MM_TM, m), _MM_TN
    if k <= 4096:
        return pl.pallas_call(
            functools.partial(_mm_kernel, act=act),
            out_shape=jax.ShapeDtypeStruct((m, n), out_dtype),
            grid=(m // tm, n // tn),
            in_specs=[
                pl.BlockSpec((tm, k), lambda i, j: (i, 0)),
                pl.BlockSpec((k, tn), lambda i, j: (0, j)),
            ],
            out_specs=pl.BlockSpec((tm, tn), lambda i, j: (i, j)),
            compiler_params=_params(("parallel", "arbitrary"), V7X_VMEM_LIMIT),
            name="matmul",
        )(a, b)
    assert act is None
    tk = _MM_TK
    return pl.pallas_call(
        _mm_acc_kernel,
        out_shape=jax.ShapeDtypeStruct((m, n), out_dtype),
        grid=(m // tm, n // tn, k // tk),
        in_specs=[
            pl.BlockSpec((tm, tk), lambda i, j, kk: (i, kk)),
            pl.BlockSpec((tk, tn), lambda i, j, kk: (kk, j)),
        ],
        out_specs=pl.BlockSpec((tm, tn), lambda i, j, kk: (i, j)),
        scratch_shapes=[pltpu.VMEM((tm, tn), _F32)],
        compiler_params=_params(("parallel", "arbitrary", "arbitrary"), V7X_VMEM_LIMIT),
        name="matmul_kacc",
    )(a, b)


def _dot_nt(a, b):
    return lax.dot_general(a, b, (((1,), (1,)), ((), ())), preferred_element_type=_F32)


def _moba_kernel(q_ref, k_ref, v_ref, o_ref, kmh_ref, kml_ref):
    i = pl.program_id(1)
    blk = MOBA_BLOCK
    pair = 2 * blk
    nb = k_ref.shape[0] // blk

    @pl.when(i == 0)
    def _():
        kmh_ref[...] = jnp.zeros_like(kmh_ref)
        kml_ref[...] = jnp.zeros_like(kml_ref)

        def body(b, carry):
            start = pl.multiple_of(b * blk, blk)
            kb = k_ref[pl.ds(start, blk), :].astype(_F32)
            mean = jnp.sum(kb, axis=0, keepdims=True) * (1.0 / blk)
            hi = mean.astype(_BF16).astype(_F32)
            kmh_ref[pl.ds(b, 1), :] = hi
            kml_ref[pl.ds(b, 1), :] = mean - hi
            return carry

        lax.fori_loop(0, nb, body, 0)

    q = q_ref[...]
    gate_t = _dot_nt(kmh_ref[...].astype(_BF16), q) + _dot_nt(kml_ref[...].astype(_BF16), q)
    bidx = lax.broadcasted_iota(jnp.int32, (LANES, pair), 0)
    qblk = 2 * i + lax.broadcasted_iota(jnp.int32, (LANES, pair), 1) // blk
    g = jnp.where(bidx < qblk, gate_t, -jnp.inf)
    sel = jnp.zeros((LANES, pair), jnp.bool_)
    for r in range(MOBA_TOPK):
        mx = jnp.max(g, axis=0, keepdims=True)
        first = jnp.min(jnp.where(g == mx, bidx, LANES), axis=0, keepdims=True)
        pick = (bidx == first) & (r < qblk)
        sel = sel | pick
        g = jnp.where(pick, -jnp.inf, g)
    unsel = jnp.where(sel | (bidx >= qblk), 0.0, 1.0).astype(_F32).T
    q_aug = jnp.concatenate([q, unsel.astype(_BF16)], axis=1)

    lane = lax.broadcasted_iota(jnp.int32, (pair, LANES), 1)
    half = lax.broadcasted_iota(jnp.int32, (pair, LANES), 0) // blk
    ones = jnp.ones((pair, LANES), _BF16)

    def scores(jp):
        start = pl.multiple_of(jp * pair, pair)
        w = jnp.where(lane == 2 * jp + half, NEG, 0.0).astype(_BF16)
        k_aug = jnp.concatenate([k_ref[pl.ds(start, pair), :], w], axis=1)
        return _dot_nt(q_aug, k_aug)

    def update(s, jp, m, acc):
        start = pl.multiple_of(jp * pair, pair)
        v_aug = jnp.concatenate([v_ref[pl.ds(start, pair), :], ones], axis=1)
        m_new = jnp.maximum(m, jnp.max(s, axis=-1, keepdims=True))
        p = jnp.exp2(s - m_new).astype(_BF16)
        acc = jnp.exp2(m - m_new) * acc + jnp.dot(p, v_aug, preferred_element_type=_F32)
        return m_new, acc

    def body(t, carry):
        s, m, acc = carry
        s_next = scores(t + 1)
        m, acc = update(s, t, m, acc)
        return s_next, m, acc

    init = (scores(0), jnp.full((pair, 1), NEG, _F32), jnp.zeros((pair, 2 * HEAD_DIM), _F32))
    s, m, acc = lax.fori_loop(0, i, body, init)

    kpos = lax.broadcasted_iota(jnp.int32, (pair, pair), 1)
    qpos = lax.broadcasted_iota(jnp.int32, (pair, pair), 0)
    m, acc = update(jnp.where(kpos <= qpos, s, NEG), i, m, acc)
    o_ref[...] = (acc[:, :HEAD_DIM] / acc[:, HEAD_DIM:]).astype(o_ref.dtype)


def _moba_attention(qkv):
    s = qkv.shape[0]
    nb = s // MOBA_BLOCK
    assert nb % 2 == 0 and nb <= LANES
    nbp = LANES
    return pl.pallas_call(
        _moba_kernel,
        out_shape=jax.ShapeDtypeStruct((s, MIX_WIDTH), _BF16),
        grid=(N_HEADS, nb // 2),
        in_specs=[
            pl.BlockSpec((2 * MOBA_BLOCK, HEAD_DIM), lambda h, i: (i, h)),
            pl.BlockSpec((s, HEAD_DIM), lambda h, i: (0, N_HEADS + h)),
            pl.BlockSpec((s, HEAD_DIM), lambda h, i: (0, 2 * N_HEADS + h)),
        ],
        out_specs=pl.BlockSpec((2 * MOBA_BLOCK, HEAD_DIM), lambda h, i: (i, h)),
        scratch_shapes=[pltpu.VMEM((nbp, HEAD_DIM), _F32), pltpu.VMEM((nbp, HEAD_DIM), _F32)],
        compiler_params=_params(("parallel", "arbitrary"), 40 * 1024 * 1024),
        name="moba_attention",
    )(qkv, qkv, qkv)


_CONV_TR = 128
_CONV_HALO = 32
_CONV_LC = 256


def _conv_kernel(cur_ref, halo_ref, w_ref, cb_ref, g_ref, b_ref, o_ref, u_ref, y_ref):
    i = pl.program_id(0)
    ch = CONV_CH
    cur = cur_ref[...]
    u_ref[_CONV_HALO:, :] = cur[:, :ch] * jax.nn.sigmoid(cur[:, ch:])
    halo = halo_ref[...]
    uh = halo[:, :ch] * jax.nn.sigmoid(halo[:, ch:])
    u_ref[:_CONV_HALO, :] = jnp.where(i > 0, uh, 0.0)

    off = _CONV_HALO - (CONV_WIDTH - 1)
    for c0 in range(0, ch, _CONV_LC):
        acc = jnp.zeros((_CONV_TR, _CONV_LC), _F32) + cb_ref[:, c0:c0 + _CONV_LC]
        for t in range(CONV_WIDTH):
            acc = acc + u_ref[off + t:off + t + _CONV_TR, c0:c0 + _CONV_LC] * w_ref[t:t + 1, c0:c0 + _CONV_LC]
        y_ref[:, c0:c0 + _CONV_LC] = acc

    y = y_ref[...]
    mu = jnp.mean(y, axis=-1, keepdims=True)
    yc = y - mu
    var = jnp.mean(yc * yc, axis=-1, keepdims=True)
    yn = yc * lax.rsqrt(var + LN_EPS) * g_ref[...] + b_ref[...]
    o_ref[...] = (yn * jax.nn.sigmoid(yn)).astype(o_ref.dtype)


def _conv_module(glu, conv_w, conv_b, ln_g, ln_b):
    s = glu.shape[0]
    ch = CONV_CH
    ratio = _CONV_TR // _CONV_HALO
    vec = lambda: pl.BlockSpec((1, ch), lambda i: (0, 0))
    return pl.pallas_call(
        _conv_kernel,
        out_shape=jax.ShapeDtypeStruct((s, ch), _BF16),
        grid=(s // _CONV_TR,),
        in_specs=[
            pl.BlockSpec((_CONV_TR, 2 * ch), lambda i: (i, 0)),
            pl.BlockSpec((_CONV_HALO, 2 * ch), lambda i: (jnp.maximum(i * ratio - 1, 0), 0)),
            pl.BlockSpec((CONV_WIDTH, ch), lambda i: (0, 0)),
            vec(), vec(), vec(),
        ],
        out_specs=pl.BlockSpec((_CONV_TR, ch), lambda i: (i, 0)),
        scratch_shapes=[pltpu.VMEM((_CONV_TR + _CONV_HALO, ch), _F32), pltpu.VMEM((_CONV_TR, ch), _F32)],
        compiler_params=_params(("parallel",), 40 * 1024 * 1024),
        name="conformer_conv",
    )(glu, glu, conv_w, conv_b.reshape(1, ch), ln_g.reshape(1, ch), ln_b.reshape(1, ch))


def _dil_kernel(q_ref, kp_ref, kc_ref, vp_ref, vc_ref, o_ref, lse_ref, *, sub, chunk):
    n = pl.program_id(1)
    rows = sub * chunk

    def pos(axis):
        rho = lax.broadcasted_iota(jnp.int32, (rows, rows), axis)
        return sub * (rho % chunk) + rho // chunk

    dist = pos(0) - pos(1)
    mask_cur = (dist >= 0) & (dist <= DIL_WINDOW)
    mask_prev = (dist + rows <= DIL_WINDOW) & (n > 0)
    lane = lax.broadcasted_iota(jnp.int32, (rows, LANES), 1)
    lse_tile = jnp.zeros((rows, LANES), _F32)
    for hh in range(N_HEADS):
        sl = slice(hh * HEAD_DIM, (hh + 1) * HEAD_DIM)
        ld = lambda ref: ref[:, :, sl].reshape(rows, HEAD_DIM)
        q = ld(q_ref)
        sp = jnp.where(mask_prev, _dot_nt(q, ld(kp_ref)), NEG)
        sc = jnp.where(mask_cur, _dot_nt(q, ld(kc_ref)), NEG)
        m = jnp.maximum(jnp.max(sp, axis=-1, keepdims=True), jnp.max(sc, axis=-1, keepdims=True))
        pp = jnp.exp2(sp - m)
        pc = jnp.exp2(sc - m)
        l = jnp.sum(pp, axis=-1, keepdims=True) + jnp.sum(pc, axis=-1, keepdims=True)
        o = (jnp.dot(pp.astype(_BF16), ld(vp_ref), preferred_element_type=_F32)
             + jnp.dot(pc.astype(_BF16), ld(vc_ref), preferred_element_type=_F32))
        o_ref[:, :, sl] = (o / l).reshape(sub, chunk, HEAD_DIM)
        lse_tile = jnp.where(lane == hh, m + jnp.log2(l), lse_tile)
    lse_ref[...] = lse_tile.reshape(sub, chunk, LANES)


def _dilated_group(qkv, group, dil):
    s, c = qkv.shape
    sub = STREAMS // dil
    chunk = max(DIL_WINDOW // sub, 16)
    ls = s // STREAMS
    view = qkv.reshape(sub, dil, ls, c)
    base = 3 * group

    def spec(slot, prev, width=MIX_WIDTH):
        col = base + slot if width == MIX_WIDTH else 0
        if prev:
            return pl.BlockSpec((sub, None, chunk, width), lambda r, n: (0, r, jnp.maximum(n - 1, 0), col))
        return pl.BlockSpec((sub, None, chunk, width), lambda r, n: (0, r, n, col))

    o, lse = pl.pallas_call(
        functools.partial(_dil_kernel, sub=sub, chunk=chunk),
        out_shape=[jax.ShapeDtypeStruct((sub, dil, ls, MIX_WIDTH), _F32),
                   jax.ShapeDtypeStruct((sub, dil, ls, LANES), _F32)],
        grid=(dil, ls // chunk),
        in_specs=[spec(0, False), spec(1, True), spec(1, False), spec(2, True), spec(2, False)],
        out_specs=[pl.BlockSpec((sub, None, chunk, MIX_WIDTH), lambda r, n: (0, r, n, 0)),
                   spec(0, False, LANES)],
        compiler_params=_params(("parallel", "arbitrary"), 40 * 1024 * 1024),
        name=f"dilated_attention_d{dil}",
    )(view, view, view, view, view)
    return o.reshape(s, MIX_WIDTH), lse.reshape(s, LANES)


def _merge_kernel(o0_ref, o1_ref, o2_ref, l0_ref, l1_ref, l2_ref, out_ref):
    flat = lambda ref: ref[...].reshape(_ROW_TM, ref.shape[-1])
    l0, l1, l2 = flat(l0_ref), flat(l1_ref), flat(l2_ref)
    mx = jnp.maximum(jnp.maximum(l0, l1), l2)
    e0, e1, e2 = jnp.exp2(l0 - mx), jnp.exp2(l1 - mx), jnp.exp2(l2 - mx)
    inv = 1.0 / (e0 + e1 + e2)
    w0, w1, w2 = e0 * inv, e1 * inv, e2 * inv
    perm = _stream_permutation()
    for hh in range(N_HEADS):
        sl = slice(hh * HEAD_DIM, (hh + 1) * HEAD_DIM)
        ld = lambda ref: ref[:, :, sl].reshape(_ROW_TM, HEAD_DIM)
        o = w0[:, hh:hh + 1] * ld(o0_ref) + w1[:, hh:hh + 1] * ld(o1_ref) + w2[:, hh:hh + 1] * ld(o2_ref)
        out_ref[:, sl] = jnp.dot(perm, o.astype(_BF16), preferred_element_type=_F32).astype(out_ref.dtype)


def _merge_groups(outs, lses):
    s = outs[0].shape[0]
    view = lambda t: t.reshape(STREAMS, s // STREAMS, t.shape[-1])
    ot = pl.BlockSpec((STREAMS, _STREAM_ROWS, MIX_WIDTH), lambda i: (0, i, 0))
    lt = pl.BlockSpec((STREAMS, _STREAM_ROWS, LANES), lambda i: (0, i, 0))
    return pl.pallas_call(
        _merge_kernel,
        out_shape=jax.ShapeDtypeStruct((s, MIX_WIDTH), _BF16),
        grid=(s // _ROW_TM,),
        in_specs=[ot, ot, ot, lt, lt, lt],
        out_specs=pl.BlockSpec((_ROW_TM, MIX_WIDTH), lambda i: (i, 0)),
        compiler_params=_params(("parallel",), 40 * 1024 * 1024),
        name="merge_dilated_groups",
    )(*[view(t) for t in outs], *[view(t) for t in lses])


def _rope_tables(pos):
    s = pos.shape[0]
    pos = pos.astype(_F32)
    inv = ROPE_THETA ** (-jnp.arange(0, ROT_DIM, 2, dtype=_F32) / ROT_DIM)
    ang = pos[:, None] * inv[None, :]
    cos, sin = jnp.cos(ang), jnp.sin(ang)
    zeros = jnp.zeros((s, HEAD_DIM - ROT_DIM), _F32)
    zh = jnp.zeros((s, ROT_HALF), _F32)
    c = jnp.concatenate([cos, cos, jnp.ones((s, HEAD_DIM - ROT_DIM), _F32)], axis=1)
    sa = jnp.concatenate([-sin, zh, zeros], axis=1)
    sb = jnp.concatenate([zh, sin, zeros], axis=1)
    return c, sa, sb


def _forward(x, c, w_ada, b_ada, w_in_ab, w_out_ab, conv_w, conv_b, conv_ln_g, conv_ln_b,
             w_in_c, w_out_c, w_ff1, w_ff2, ln_g, ln_b):
    s, d = x.shape
    rope = _rope_tables(jnp.arange(s))
    rope_streams = _rope_tables(jnp.arange(s).reshape(s // STREAMS, STREAMS).T.reshape(s))
    mod = _ada_modulation(c.reshape(d), w_ada, b_ada)
    bf = lambda t: t.astype(_BF16)

    h = _modulate(x, mod, 0, 1)
    w_in = w_in_ab[0]
    qkv = _matmul_qkv(h, bf(w_in[:, :3 * MIX_WIDTH]), *rope)
    glu = _matmul(h, bf(w_in[:, 3 * MIX_WIDTH:]), _F32)
    a_out = _moba_attention(qkv)
    u = _conv_module(glu, conv_w[0], conv_b[0], conv_ln_g[0], conv_ln_b[0])
    y = _matmul(jnp.concatenate([a_out, u], axis=1), bf(w_out_ab[0]), _F32)
    x, h = _residual_ln(x, y, mod, 2, ln_g[0, 0], ln_b[0, 0], (3, 4))
    y = _matmul(_matmul(h, bf(w_ff1[0]), _BF16, act="sqrelu"), bf(w_ff2[0]), _F32)
    x, h = _residual_ln(x, y, mod, 5, ln_g[0, 1], ln_b[0, 1], (6, 7), stream_h=True)

    qkv = _matmul_qkv(h, bf(w_in_c[0]), *rope_streams)
    outs, lses = [], []
    for gi, (_, dil) in enumerate(DIL_CONFIGS):
        o, lse = _dilated_group(qkv, gi, dil)
        outs.append(o)
        lses.append(lse)
    y = _matmul(_merge_groups(outs, lses), bf(w_out_c[0]), _F32)
    x, h = _residual_ln(x, y, mod, 8, ln_g[1, 0], ln_b[1, 0], (9, 10))
    y = _matmul(_matmul(h, bf(w_ff1[1]), _BF16, act="sqrelu"), bf(w_ff2[1]), _F32)
    x, _ = _residual_ln(x, y, mod, 11, ln_g[1, 1], ln_b[1, 1])
    return x


def kernel(x, c, w_ada, b_ada, w_in_ab, w_out_ab, conv_w, conv_b, conv_ln_g, conv_ln_b,
           w_in_c, w_out_c, w_ff1, w_ff2, ln_g, ln_b):
    b, s, d = x.shape
    assert b == 1 and d == D_MODEL and s % (DIL_CONFIGS[-1][1] * DIL_WINDOW) == 0
    out = _forward(x.reshape(s, d), c, w_ada, b_ada, w_in_ab, w_out_ab, conv_w, conv_b,
                   conv_ln_g, conv_ln_b, w_in_c, w_out_c, w_ff1, w_ff2, ln_g, ln_b)
    return out.reshape(b, s, d)
```

```python
import functools

import jax
import jax.numpy as jnp
from jax import lax
from jax.experimental import pallas as pl
from jax.experimental.pallas import tpu as pltpu

D_MODEL = 4096
DEPTH = 2
HEAD_DIM = 128
ROT_DIM = HEAD_DIM // 4
ROT_HALF = ROT_DIM // 2
ROPE_THETA = 500000.0
N_HEADS = D_MODEL // (2 * HEAD_DIM)
MIX_WIDTH = N_HEADS * HEAD_DIM
MOBA_BLOCK = 256
MOBA_TOPK = 3
CONV_CH = D_MODEL // 2
CONV_WIDTH = 31
DIL_CONFIGS = ((128, 1), (512, 4), (2048, 16))
DIL_WINDOW = 128
D_FF = 4 * D_MODEL
ALPHA = (2 * DEPTH) ** 0.25
LN_EPS = 1e-5
SM_SCALE = HEAD_DIM ** -0.5
LOG2E = 1.4426950408889634
Q_SCALE = SM_SCALE * LOG2E
STREAMS = DIL_CONFIGS[-1][1]

LANES = 128
V7X_VMEM_LIMIT = 56 * 1024 * 1024
NEG = -1e30

_BF16 = jnp.bfloat16
_F32 = jnp.float32


def _params(sem, vmem=None):
    return pltpu.CompilerParams(dimension_semantics=sem, vmem_limit_bytes=vmem)


_ADA_TN = 512
_ADA_ROWS = 512


def _ada_kernel(c_ref, w_ref, b_ref, o_ref, cond_ref):
    @pl.when((pl.program_id(0) == 0) & (pl.program_id(1) == 0))
    def _():
        cv = c_ref[...]
        cond_ref[...] = cv * jax.nn.sigmoid(cv)

    d = w_ref.shape[0]
    acc = jnp.zeros((8, _ADA_TN), _F32)
    for r0 in range(0, d, _ADA_ROWS):
        cond = cond_ref[r0:r0 + _ADA_ROWS, :]
        w = w_ref[r0:r0 + _ADA_ROWS, :]
        cond_t = jnp.concatenate([cond] * (_ADA_TN // LANES), axis=1)
        prod = (w * cond_t).reshape(_ADA_ROWS // 8, 8, _ADA_TN)
        acc = acc + jnp.sum(prod, axis=0)
    o_ref[...] = jnp.sum(acc, axis=0, keepdims=True) + b_ref[...]


def _ada_modulation(c, w_ada, b_ada):
    depth, d, n = w_ada.shape
    c_rep = jnp.broadcast_to(c.reshape(d, 1), (d, LANES))
    out = pl.pallas_call(
        _ada_kernel,
        out_shape=jax.ShapeDtypeStruct((depth, 1, n), _F32),
        grid=(depth, n // _ADA_TN),
        in_specs=[
            pl.BlockSpec((d, LANES), lambda l, j: (0, 0)),
            pl.BlockSpec((None, d, _ADA_TN), lambda l, j: (l, 0, j)),
            pl.BlockSpec((None, 1, _ADA_TN), lambda l, j: (l, 0, j)),
        ],
        out_specs=pl.BlockSpec((None, 1, _ADA_TN), lambda l, j: (l, 0, j)),
        scratch_shapes=[pltpu.VMEM((d, LANES), _F32)],
        compiler_params=_params(("arbitrary", "arbitrary"), 40 * 1024 * 1024),
        name="ada_modulation",
    )(c_rep, w_ada, b_ada.reshape(depth, 1, n))
    return out.reshape(depth * 6, 1, d)


_ROW_TM = 256


def _modulate_kernel(x_ref, shift_ref, scale_ref, h_ref):
    h_ref[...] = (x_ref[...] * (1.0 + scale_ref[...]) + shift_ref[...]).astype(h_ref.dtype)


def _modulate(x, mod, shift_row, scale_row):
    s, d = x.shape
    return pl.pallas_call(
        _modulate_kernel,
        out_shape=jax.ShapeDtypeStruct((s, d), _BF16),
        grid=(s // _ROW_TM,),
        in_specs=[
            pl.BlockSpec((_ROW_TM, d), lambda i: (i, 0)),
            pl.BlockSpec((None, 1, d), lambda i: (shift_row, 0, 0)),
            pl.BlockSpec((None, 1, d), lambda i: (scale_row, 0, 0)),
        ],
        out_specs=pl.BlockSpec((_ROW_TM, d), lambda i: (i, 0)),
        compiler_params=_params(("parallel",)),
        name="modulate",
    )(x, mod, mod)


_STREAM_ROWS = _ROW_TM // STREAMS


def _stream_permutation():
    assert _STREAM_ROWS == STREAMS
    dst = lax.broadcasted_iota(jnp.int32, (_ROW_TM, _ROW_TM), 0)
    src = lax.broadcasted_iota(jnp.int32, (_ROW_TM, _ROW_TM), 1)
    return (src == (dst % STREAMS) * STREAMS + dst // STREAMS).astype(_BF16)


def _ln_kernel(x_ref, y_ref, gate_ref, g_ref, b_ref, *rest, with_h, stream_h):
    if with_h:
        shift_ref, scale_ref, xo_ref, h_ref = rest
    else:
        (xo_ref,) = rest
    z = ALPHA * x_ref[...] + (1.0 + gate_ref[...]) * y_ref[...]
    mu = jnp.mean(z, axis=-1, keepdims=True)
    zc = z - mu
    var = jnp.mean(zc * zc, axis=-1, keepdims=True)
    xn = zc * lax.rsqrt(var + LN_EPS) * g_ref[...] + b_ref[...]
    xo_ref[...] = xn
    if with_h:
        h = (xn * (1.0 + scale_ref[...]) + shift_ref[...]).astype(h_ref.dtype)
        if stream_h:
            h = jnp.dot(_stream_permutation(), h, preferred_element_type=_F32).astype(h_ref.dtype)
            h = h.reshape(h_ref.shape)
        h_ref[...] = h


def _residual_ln(x, y, mod, gate_row, ln_g, ln_b, next_rows=None, stream_h=False):
    s, d = x.shape
    with_h = next_rows is not None
    row = lambda r: pl.BlockSpec((None, 1, d), lambda i: (r, 0, 0))
    vec = pl.BlockSpec((1, d), lambda i: (0, 0))
    tile = pl.BlockSpec((_ROW_TM, d), lambda i: (i, 0))
    in_specs = [tile, tile, row(gate_row), vec, vec]
    args = [x, y, mod, ln_g.reshape(1, d), ln_b.reshape(1, d)]
    out_shape = [jax.ShapeDtypeStruct((s, d), _F32)]
    out_specs = [tile]
    if with_h:
        in_specs += [row(next_rows[0]), row(next_rows[1])]
        args += [mod, mod]
        if stream_h:
            out_shape.append(jax.ShapeDtypeStruct((STREAMS, s // STREAMS, d), _BF16))
            out_specs.append(pl.BlockSpec((STREAMS, _STREAM_ROWS, d), lambda i: (0, i, 0)))
        else:
            out_shape.append(jax.ShapeDtypeStruct((s, d), _BF16))
            out_specs.append(tile)
    res = pl.pallas_call(
        functools.partial(_ln_kernel, with_h=with_h, stream_h=stream_h),
        out_shape=out_shape,
        grid=(s // _ROW_TM,),
        in_specs=in_specs,
        out_specs=out_specs,
        compiler_params=_params(("parallel",), 40 * 1024 * 1024),
        name="residual_ln",
    )(*args)
    return (res[0], res[1].reshape(s, d)) if with_h else (res[0], None)


_MM_TM = 1024
_MM_TN = 1024
_MM_TK = 2048
_MM_CHUNK = 256


def _rope_tile(y, c, sa, sb):
    return y * c + pltpu.roll(y, LANES - ROT_HALF, 1) * sa + pltpu.roll(y, ROT_HALF, 1) * sb


def _mm_qkv_kernel(a_ref, b_ref, c_ref, sa_ref, sb_ref, o_ref):
    tn = o_ref.shape[1]
    which = ((pl.program_id(1) * tn) // MIX_WIDTH) % 3
    is_v = which == 2
    qs = jnp.where(which == 0, Q_SCALE, 1.0).astype(_F32)
    c = jnp.where(is_v, 1.0, c_ref[...] * qs)
    sa = jnp.where(is_v, 0.0, sa_ref[...] * qs)
    sb = jnp.where(is_v, 0.0, sb_ref[...] * qs)
    for c0 in range(0, tn, _MM_CHUNK):
        acc = jnp.dot(a_ref[...], b_ref[:, c0:c0 + _MM_CHUNK], preferred_element_type=_F32)
        for h0 in range(0, _MM_CHUNK, HEAD_DIM):
            y = _rope_tile(acc[:, h0:h0 + HEAD_DIM], c, sa, sb)
            o_ref[:, c0 + h0:c0 + h0 + HEAD_DIM] = y.astype(o_ref.dtype)


def _matmul_qkv(a, b, rope_c, rope_sa, rope_sb):
    m, k = a.shape
    n = b.shape[1]
    tm, tn = min(_MM_TM, m), _MM_TN
    tab = pl.BlockSpec((tm, LANES), lambda i, j: (i, 0))
    return pl.pallas_call(
        _mm_qkv_kernel,
        out_shape=jax.ShapeDtypeStruct((m, n), _BF16),
        grid=(m // tm, n // tn),
        in_specs=[
            pl.BlockSpec((tm, k), lambda i, j: (i, 0)),
            pl.BlockSpec((k, tn), lambda i, j: (0, j)),
            tab, tab, tab,
        ],
        out_specs=pl.BlockSpec((tm, tn), lambda i, j: (i, j)),
        compiler_params=_params(("parallel", "arbitrary"), V7X_VMEM_LIMIT),
        name="matmul_qkv_rope",
    )(a, b, rope_c, rope_sa, rope_sb)


def _mm_kernel(*refs, act):
    *a_refs, b_ref, o_ref = refs
    acc, k0 = None, 0
    for a_ref in a_refs:
        k1 = k0 + a_ref.shape[1]
        part = jnp.dot(a_ref[...], b_ref[k0:k1, :], preferred_element_type=_F32)
        acc = part if acc is None else acc + part
        k0 = k1
    if act == "sqrelu":
        acc = jnp.square(jnp.maximum(acc, 0.0))
    o_ref[...] = acc.astype(o_ref.dtype)


def _mm_acc_kernel(a_ref, b_ref, o_ref, acc_ref):
    kk = pl.program_id(2)

    @pl.when(kk == 0)
    def _():
        acc_ref[...] = jnp.zeros_like(acc_ref)

    acc_ref[...] += jnp.dot(a_ref[...], b_ref[...], preferred_element_type=_F32)

    @pl.when(kk == pl.num_programs(2) - 1)
    def _():
        o_ref[...] = acc_ref[...].astype(o_ref.dtype)


def _matmul(a, b, out_dtype, act=None):
    parts = a if isinstance(a, tuple) else (a,)
    m = parts[0].shape[0]
    k, n = b.shape
    assert sum(p.shape[1] for p in parts) == k
    tm, tn = min(_MM_TM, m), _MM_TN
    if k <= 4096:
        return pl.pallas_call(
            functools.partial(_mm_kernel, act=act),
            out_shape=jax.ShapeDtypeStruct((m, n), out_dtype),
            grid=(m // tm, n // tn),
            in_specs=[pl.BlockSpec((tm, p.shape[1]), lambda i, j: (i, 0)) for p in parts]
            + [pl.BlockSpec((k, tn), lambda i, j: (0, j))],
            out_specs=pl.BlockSpec((tm, tn), lambda i, j: (i, j)),
            compiler_params=_params(("parallel", "arbitrary"), V7X_VMEM_LIMIT),
            name="matmul",
        )(*parts, b)
    assert act is None and len(parts) == 1
    a = parts[0]
    tk = _MM_TK
    return pl.pallas_call(
        _mm_acc_kernel,
        out_shape=jax.ShapeDtypeStruct((m, n), out_dtype),
        grid=(m // tm, n // tn, k // tk),
        in_specs=[
            pl.BlockSpec((tm, tk), lambda i, j, kk: (i, kk)),
            pl.BlockSpec((tk, tn), lambda i, j, kk: (kk, j)),
        ],
        out_specs=pl.BlockSpec((tm, tn), lambda i, j, kk: (i, j)),
        scratch_shapes=[pltpu.VMEM((tm, tn), _F32)],
        compiler_params=_params(("parallel", "arbitrary", "arbitrary"), V7X_VMEM_LIMIT),
        name="matmul_kacc",
    )(a, b)


def _dot_nt(a, b):
    return lax.dot_general(a, b, (((1,), (1,)), ((), ())), preferred_element_type=_F32)


_MOBA_STRIP = 32


def _moba_kernel(q_ref, k_ref, v_ref, o_ref, kmh_ref, kml_ref, s0_ref, s1_ref, p_ref, m_ref, alpha_ref,
                 acc_ref):
    i = pl.program_id(1)
    blk = MOBA_BLOCK
    pair = 2 * blk
    nb = k_ref.shape[0] // blk

    @pl.when(i == 0)
    def _():
        kmh_ref[...] = jnp.zeros_like(kmh_ref)
        kml_ref[...] = jnp.zeros_like(kml_ref)

        def body(b, carry):
            start = pl.multiple_of(b * blk, blk)
            kb = k_ref[pl.ds(start, blk), :].astype(_F32)
            mean = jnp.sum(kb, axis=0, keepdims=True) * (1.0 / blk)
            hi = mean.astype(_BF16).astype(_F32)
            kmh_ref[pl.ds(b, 1), :] = hi
            kml_ref[pl.ds(b, 1), :] = mean - hi
            return carry

        lax.fori_loop(0, nb, body, 0)

    q = q_ref[...]
    gate_t = _dot_nt(kmh_ref[...].astype(_BF16), q) + _dot_nt(kml_ref[...].astype(_BF16), q)
    bidx = lax.broadcasted_iota(jnp.int32, (LANES, pair), 0)
    qblk = 2 * i + lax.broadcasted_iota(jnp.int32, (LANES, pair), 1) // blk
    g = jnp.where(bidx < qblk, gate_t, -jnp.inf)
    sel = jnp.zeros((LANES, pair), jnp.bool_)
    for r in range(MOBA_TOPK):
        mx = jnp.max(g, axis=0, keepdims=True)
        first = jnp.min(jnp.where(g == mx, bidx, LANES), axis=0, keepdims=True)
        pick = (bidx == first) & (r < qblk)
        sel = sel | pick
        g = jnp.where(pick, -jnp.inf, g)
    unsel = jnp.where(sel | (bidx >= qblk), 0.0, 1.0).astype(_F32).T
    q_aug = jnp.concatenate([q, unsel.astype(_BF16)], axis=1)

    lane = lax.broadcasted_iota(jnp.int32, (pair, LANES), 1)
    half = lax.broadcasted_iota(jnp.int32, (pair, LANES), 0) // blk
    ones = jnp.ones((pair, LANES), _BF16)

    def scores_to(slot, jp):
        start = pl.multiple_of(jp * pair, pair)
        w = jnp.where(lane == 2 * jp + half, NEG, 0.0).astype(_BF16)
        k_aug = jnp.concatenate([k_ref[pl.ds(start, pair), :], w], axis=1)
        (s0_ref, s1_ref)[slot][...] = _dot_nt(q_aug, k_aug)

    def update(slot, jp, causal):
        for r0 in range(0, pair, _MOBA_STRIP):
            rows = slice(r0, r0 + _MOBA_STRIP)
            s = (s0_ref, s1_ref)[slot][rows, :]
            if causal:
                kpos = lax.broadcasted_iota(jnp.int32, (_MOBA_STRIP, pair), 1)
                qpos = r0 + lax.broadcasted_iota(jnp.int32, (_MOBA_STRIP, pair), 0)
                s = jnp.where(kpos <= qpos, s, NEG)
            m_old = m_ref[rows, :]
            m_new = jnp.maximum(m_old, jnp.max(s, axis=-1, keepdims=True))
            m_ref[rows, :] = m_new
            alpha_ref[rows, :] = jnp.exp2(m_old - m_new)
            p_ref[rows, :] = jnp.exp2(s - m_new).astype(_BF16)
        start = pl.multiple_of(jp * pair, pair)
        v_aug = jnp.concatenate([v_ref[pl.ds(start, pair), :], ones], axis=1)
        acc_ref[...] = alpha_ref[...] * acc_ref[...] + jnp.dot(p_ref[...], v_aug, preferred_element_type=_F32)

    def body(u, carry):
        scores_to(1, 2 * u + 1)
        update(0, 2 * u, False)
        scores_to(0, 2 * u + 2)
        update(1, 2 * u + 1, False)
        return carry

    m_ref[...] = jnp.full(m_ref.shape, NEG, _F32)
    acc_ref[...] = jnp.zeros(acc_ref.shape, _F32)
    scores_to(0, 0)
    lax.fori_loop(0, i // 2, body, 0)

    @pl.when(i % 2 == 0)
    def _():
        update(0, i, True)

    @pl.when(i % 2 == 1)
    def _():
        scores_to(1, i)
        update(0, i - 1, False)
        update(1, i, True)

    o_ref[...] = (acc_ref[:, :HEAD_DIM] / acc_ref[:, HEAD_DIM:]).astype(o_ref.dtype)


def _moba_attention(qkv):
    s = qkv.shape[0]
    nb = s // MOBA_BLOCK
    assert nb % 2 == 0 and nb <= LANES
    nbp = LANES
    pair = 2 * MOBA_BLOCK
    return pl.pallas_call(
        _moba_kernel,
        out_shape=jax.ShapeDtypeStruct((s, MIX_WIDTH), _BF16),
        grid=(N_HEADS, nb // 2),
        in_specs=[
            pl.BlockSpec((2 * MOBA_BLOCK, HEAD_DIM), lambda h, i: (i, h)),
            pl.BlockSpec((s, HEAD_DIM), lambda h, i: (0, N_HEADS + h)),
            pl.BlockSpec((s, HEAD_DIM), lambda h, i: (0, 2 * N_HEADS + h)),
        ],
        out_specs=pl.BlockSpec((2 * MOBA_BLOCK, HEAD_DIM), lambda h, i: (i, h)),
        scratch_shapes=[pltpu.VMEM((nbp, HEAD_DIM), _F32), pltpu.VMEM((nbp, HEAD_DIM), _F32),
                        pltpu.VMEM((pair, pair), _F32),
                        pltpu.VMEM((pair, pair), _F32),
                        pltpu.VMEM((pair, pair), _BF16),
                        pltpu.VMEM((pair, 1), _F32),
                        pltpu.VMEM((pair, 1), _F32),
                        pltpu.VMEM((pair, 2 * HEAD_DIM), _F32)],
        compiler_params=_params(("parallel", "arbitrary"), 40 * 1024 * 1024),
        name="moba_attention",
    )(qkv, qkv, qkv)


_CONV_TR = 128
_CONV_HALO = 32
_CONV_LC = 128


def _conv_kernel(cur_ref, halo_ref, w_ref, cb_ref, g_ref, b_ref, o_ref, u_ref, y_ref, shift_ref):
    i = pl.program_id(0)
    ch = CONV_CH
    cur = cur_ref[...]
    u_ref[_CONV_HALO:, :] = cur[:, :ch] * jax.nn.sigmoid(cur[:, ch:])
    halo = halo_ref[...]
    uh = halo[:, :ch] * jax.nn.sigmoid(halo[:, ch:])
    u_ref[:_CONV_HALO, :] = jnp.where(i > 0, uh, 0.0)

    off = _CONV_HALO - (CONV_WIDTH - 1)
    for c0 in range(0, ch, _CONV_LC):
        cols = slice(c0, c0 + _CONV_LC)
        vshape = (_CONV_TR // 8, 8, _CONV_LC)
        acc = jnp.zeros(vshape, _F32) + jnp.broadcast_to(cb_ref[:, cols], vshape[1:])
        for b in range(8):
            a_max = (CONV_WIDTH - 1 - b) // 8
            span = 8 * a_max + _CONV_TR
            ub_ref = shift_ref.at[b % 2]
            ub_ref[0:span, :] = u_ref[off + b:off + b + span, cols]
            for a in range(a_max + 1):
                t = 8 * a + b
                wt = jnp.broadcast_to(w_ref[t:t + 1, cols], vshape[1:])
                acc = acc + ub_ref[8 * a:8 * a + _CONV_TR, :].reshape(vshape) * wt
        y_ref[:, cols] = acc.reshape(_CONV_TR, _CONV_LC)

    y = y_ref[...]
    mu = jnp.mean(y, axis=-1, keepdims=True)
    yc = y - mu
    var = jnp.mean(yc * yc, axis=-1, keepdims=True)
    yn = yc * lax.rsqrt(var + LN_EPS) * g_ref[...] + b_ref[...]
    o_ref[...] = (yn * jax.nn.sigmoid(yn)).astype(o_ref.dtype)


def _conv_module(glu, conv_w, conv_b, ln_g, ln_b):
    s = glu.shape[0]
    ch = CONV_CH
    ratio = _CONV_TR // _CONV_HALO
    vec = lambda: pl.BlockSpec((1, ch), lambda i: (0, 0))
    return pl.pallas_call(
        _conv_kernel,
        out_shape=jax.ShapeDtypeStruct((s, ch), _BF16),
        grid=(s // _CONV_TR,),
        in_specs=[
            pl.BlockSpec((_CONV_TR, 2 * ch), lambda i: (i, 0)),
            pl.BlockSpec((_CONV_HALO, 2 * ch), lambda i: (jnp.maximum(i * ratio - 1, 0), 0)),
            pl.BlockSpec((CONV_WIDTH, ch), lambda i: (0, 0)),
            vec(), vec(), vec(),
        ],
        out_specs=pl.BlockSpec((_CONV_TR, ch), lambda i: (i, 0)),
        scratch_shapes=[pltpu.VMEM((_CONV_TR + _CONV_HALO, ch), _F32), pltpu.VMEM((_CONV_TR, ch), _F32),
                        pltpu.VMEM((2, _CONV_TR + _CONV_HALO, _CONV_LC), _F32)],
        compiler_params=_params(("parallel",), 40 * 1024 * 1024),
        name="conformer_conv",
    )(glu, glu, conv_w, conv_b.reshape(1, ch), ln_g.reshape(1, ch), ln_b.reshape(1, ch))


def _dil_kernel(q_ref, kp_ref, kc_ref, vp_ref, vc_ref, o_ref, lse_ref, *, sub, chunk):
    n = pl.program_id(1)
    rows = sub * chunk

    def pos(axis):
        rho = lax.broadcasted_iota(jnp.int32, (rows, rows), axis)
        return sub * (rho % chunk) + rho // chunk

    dist = pos(0) - pos(1)
    mask_cur = (dist >= 0) & (dist <= DIL_WINDOW)
    mask_prev = (dist + rows <= DIL_WINDOW) & (n > 0)
    mask = jnp.concatenate([mask_prev, mask_cur], axis=1)
    lane = lax.broadcasted_iota(jnp.int32, (rows, LANES), 1)
    ones = jnp.ones((2 * rows, LANES), _BF16)
    lse_tile = jnp.zeros((rows, LANES), _F32)
    for hh in range(N_HEADS):
        sl = slice(hh * HEAD_DIM, (hh + 1) * HEAD_DIM)
        ld = lambda ref: ref[:, :, sl].reshape(rows, HEAD_DIM)
        k = jnp.concatenate([ld(kp_ref), ld(kc_ref)], axis=0)
        v = jnp.concatenate([jnp.concatenate([ld(vp_ref), ld(vc_ref)], axis=0), ones], axis=1)
        s = jnp.where(mask, _dot_nt(ld(q_ref), k), NEG)
        m = jnp.max(s, axis=-1, keepdims=True)
        ov = jnp.dot(jnp.exp2(s - m).astype(_BF16), v, preferred_element_type=_F32)
        l = ov[:, HEAD_DIM:]
        o_ref[:, :, sl] = (ov[:, :HEAD_DIM] / l).reshape(sub, chunk, HEAD_DIM)
        lse_tile = jnp.where(lane == hh, m + jnp.log2(l), lse_tile)
    lse_ref[...] = lse_tile.reshape(sub, chunk, LANES)


def _dilated_group(qkv, group, dil):
    s, c = qkv.shape
    sub = STREAMS // dil
    chunk = max(DIL_WINDOW // sub, 16)
    ls = s // STREAMS
    view = qkv.reshape(sub, dil, ls, c)
    base = 3 * group

    def spec(slot, prev, width=MIX_WIDTH):
        col = base + slot if width == MIX_WIDTH else 0
        if prev:
            return pl.BlockSpec((sub, None, chunk, width), lambda r, n: (0, r, jnp.maximum(n - 1, 0), col))
        return pl.BlockSpec((sub, None, chunk, width), lambda r, n: (0, r, n, col))

    o, lse = pl.pallas_call(
        functools.partial(_dil_kernel, sub=sub, chunk=chunk),
        out_shape=[jax.ShapeDtypeStruct((sub, dil, ls, MIX_WIDTH), _F32),
                   jax.ShapeDtypeStruct((sub, dil, ls, LANES), _F32)],
        grid=(dil, ls // chunk),
        in_specs=[spec(0, False), spec(1, True), spec(1, False), spec(2, True), spec(2, False)],
        out_specs=[pl.BlockSpec((sub, None, chunk, MIX_WIDTH), lambda r, n: (0, r, n, 0)),
                   spec(0, False, LANES)],
        compiler_params=_params(("parallel", "arbitrary"), 40 * 1024 * 1024),
        name=f"dilated_attention_d{dil}",
    )(view, view, view, view, view)
    return o.reshape(s, MIX_WIDTH), lse.reshape(s, LANES)


def _merge_kernel(o0_ref, o1_ref, o2_ref, l0_ref, l1_ref, l2_ref, out_ref):
    flat = lambda ref: ref[...].reshape(_ROW_TM, ref.shape[-1])
    l0, l1, l2 = flat(l0_ref), flat(l1_ref), flat(l2_ref)
    mx = jnp.maximum(jnp.maximum(l0, l1), l2)
    e0, e1, e2 = jnp.exp2(l0 - mx), jnp.exp2(l1 - mx), jnp.exp2(l2 - mx)
    inv = 1.0 / (e0 + e1 + e2)
    w0, w1, w2 = e0 * inv, e1 * inv, e2 * inv
    perm = _stream_permutation()
    for hh in range(N_HEADS):
        sl = slice(hh * HEAD_DIM, (hh + 1) * HEAD_DIM)
        ld = lambda ref: ref[:, :, sl].reshape(_ROW_TM, HEAD_DIM)
        o = w0[:, hh:hh + 1] * ld(o0_ref) + w1[:, hh:hh + 1] * ld(o1_ref) + w2[:, hh:hh + 1] * ld(o2_ref)
        out_ref[:, sl] = jnp.dot(perm, o.astype(_BF16), preferred_element_type=_F32).astype(out_ref.dtype)


def _merge_groups(outs, lses):
    s = outs[0].shape[0]
    view = lambda t: t.reshape(STREAMS, s // STREAMS, t.shape[-1])
    ot = pl.BlockSpec((STREAMS, _STREAM_ROWS, MIX_WIDTH), lambda i: (0, i, 0))
    lt = pl.BlockSpec((STREAMS, _STREAM_ROWS, LANES), lambda i: (0, i, 0))
    return pl.pallas_call(
        _merge_kernel,
        out_shape=jax.ShapeDtypeStruct((s, MIX_WIDTH), _BF16),
        grid=(s // _ROW_TM,),
        in_specs=[ot, ot, ot, lt, lt, lt],
        out_specs=pl.BlockSpec((_ROW_TM, MIX_WIDTH), lambda i: (i, 0)),
        compiler_params=_params(("parallel",), 40 * 1024 * 1024),
        name="merge_dilated_groups",
    )(*[view(t) for t in outs], *[view(t) for t in lses])


def _rope_tables(pos):
    s = pos.shape[0]
    pos = pos.astype(_F32)
    inv = ROPE_THETA ** (-jnp.arange(0, ROT_DIM, 2, dtype=_F32) / ROT_DIM)
    ang = pos[:, None] * inv[None, :]
    cos, sin = jnp.cos(ang), jnp.sin(ang)
    zeros = jnp.zeros((s, HEAD_DIM - ROT_DIM), _F32)
    zh = jnp.zeros((s, ROT_HALF), _F32)
    c = jnp.concatenate([cos, cos, jnp.ones((s, HEAD_DIM - ROT_DIM), _F32)], axis=1)
    sa = jnp.concatenate([-sin, zh, zeros], axis=1)
    sb = jnp.concatenate([zh, sin, zeros], axis=1)
    return c, sa, sb


def _forward(x, c, w_ada, b_ada, w_in_ab, w_out_ab, conv_w, conv_b, conv_ln_g, conv_ln_b,
             w_in_c, w_out_c, w_ff1, w_ff2, ln_g, ln_b):
    s, d = x.shape
    rope = _rope_tables(jnp.arange(s))
    rope_streams = _rope_tables(jnp.arange(s).reshape(s // STREAMS, STREAMS).T.reshape(s))
    mod = _ada_modulation(c.reshape(d), w_ada, b_ada)
    bf = lambda t: t.astype(_BF16)

    h = _modulate(x, mod, 0, 1)
    w_in = w_in_ab[0]
    qkv = _matmul_qkv(h, bf(w_in[:, :3 * MIX_WIDTH]), *rope)
    glu = _matmul(h, bf(w_in[:, 3 * MIX_WIDTH:]), _F32)
    a_out = _moba_attention(qkv)
    u = _conv_module(glu, conv_w[0], conv_b[0], conv_ln_g[0], conv_ln_b[0])
    y = _matmul((a_out, u), bf(w_out_ab[0]), _F32)
    x, h = _residual_ln(x, y, mod, 2, ln_g[0, 0], ln_b[0, 0], (3, 4))
    y = _matmul(_matmul(h, bf(w_ff1[0]), _BF16, act="sqrelu"), bf(w_ff2[0]), _F32)
    x, h = _residual_ln(x, y, mod, 5, ln_g[0, 1], ln_b[0, 1], (6, 7), stream_h=True)

    qkv = _matmul_qkv(h, bf(w_in_c[0]), *rope_streams)
    outs, lses = [], []
    for gi, (_, dil) in enumerate(DIL_CONFIGS):
        o, lse = _dilated_group(qkv, gi, dil)
        outs.append(o)
        lses.append(lse)
    y = _matmul(_merge_groups(outs, lses), bf(w_out_c[0]), _F32)
    x, h = _residual_ln(x, y, mod, 8, ln_g[1, 0], ln_b[1, 0], (9, 10))
    y = _matmul(_matmul(h, bf(w_ff1[1]), _BF16, act="sqrelu"), bf(w_ff2[1]), _F32)
    x, _ = _residual_ln(x, y, mod, 11, ln_g[1, 1], ln_b[1, 1])
    return x


def kernel(x, c, w_ada, b_ada, w_in_ab, w_out_ab, conv_w, conv_b, conv_ln_g, conv_ln_b,
           w_in_c, w_out_c, w_ff1, w_ff2, ln_g, ln_b):
    b, s, d = x.shape
    assert b == 1 and d == D_MODEL and s % (DIL_CONFIGS[-1][1] * DIL_WINDOW) == 0
    out = _forward(x.reshape(s, d), c, w_ada, b_ada, w_in_ab, w_out_ab, conv_w, conv_b,
                   conv_ln_g, conv_ln_b, w_in_c, w_out_c, w_ff1, w_ff2, ln_g, ln_b)
    return out.reshape(b, s, d)
```

```python
import functools

import jax
import jax.numpy as jnp
from jax import lax
from jax.experimental import pallas as pl
from jax.experimental.pallas import tpu as pltpu

D_MODEL = 4096
DEPTH = 2
HEAD_DIM = 128
ROT_DIM = HEAD_DIM // 4
ROT_HALF = ROT_DIM // 2
ROPE_THETA = 500000.0
N_HEADS = D_MODEL // (2 * HEAD_DIM)
MIX_WIDTH = N_HEADS * HEAD_DIM
MOBA_BLOCK = 256
MOBA_TOPK = 3
CONV_CH = D_MODEL // 2
CONV_WIDTH = 31
DIL_CONFIGS = ((128, 1), (512, 4), (2048, 16))
DIL_WINDOW = 128
D_FF = 4 * D_MODEL
ALPHA = (2 * DEPTH) ** 0.25
LN_EPS = 1e-5
SM_SCALE = HEAD_DIM ** -0.5
LOG2E = 1.4426950408889634
Q_SCALE = SM_SCALE * LOG2E
STREAMS = DIL_CONFIGS[-1][1]

LANES = 128
V7X_VMEM_LIMIT = 56 * 1024 * 1024
NEG = -1e30

_BF16 = jnp.bfloat16
_F32 = jnp.float32


def _params(sem, vmem=None):
    return pltpu.CompilerParams(dimension_semantics=sem, vmem_limit_bytes=vmem)


_ADA_TN = 512
_ADA_ROWS = 512


def _ada_kernel(c_ref, w_ref, b_ref, o_ref, cond_ref):
    @pl.when((pl.program_id(0) == 0) & (pl.program_id(1) == 0))
    def _():
        cv = c_ref[...]
        cond_ref[...] = cv * jax.nn.sigmoid(cv)

    d = w_ref.shape[0]
    acc = jnp.zeros((8, _ADA_TN), _F32)
    for r0 in range(0, d, _ADA_ROWS):
        cond = cond_ref[r0:r0 + _ADA_ROWS, :]
        w = w_ref[r0:r0 + _ADA_ROWS, :]
        cond_t = jnp.concatenate([cond] * (_ADA_TN // LANES), axis=1)
        prod = (w * cond_t).reshape(_ADA_ROWS // 8, 8, _ADA_TN)
        acc = acc + jnp.sum(prod, axis=0)
    o_ref[...] = jnp.sum(acc, axis=0, keepdims=True) + b_ref[...]


def _ada_modulation(c, w_ada, b_ada):
    depth, d, n = w_ada.shape
    c_rep = jnp.broadcast_to(c.reshape(d, 1), (d, LANES))
    out = pl.pallas_call(
        _ada_kernel,
        out_shape=jax.ShapeDtypeStruct((depth, 1, n), _F32),
        grid=(depth, n // _ADA_TN),
        in_specs=[
            pl.BlockSpec((d, LANES), lambda l, j: (0, 0)),
            pl.BlockSpec((None, d, _ADA_TN), lambda l, j: (l, 0, j)),
            pl.BlockSpec((None, 1, _ADA_TN), lambda l, j: (l, 0, j)),
        ],
        out_specs=pl.BlockSpec((None, 1, _ADA_TN), lambda l, j: (l, 0, j)),
        scratch_shapes=[pltpu.VMEM((d, LANES), _F32)],
        compiler_params=_params(("arbitrary", "arbitrary"), 40 * 1024 * 1024),
        name="ada_modulation",
    )(c_rep, w_ada, b_ada.reshape(depth, 1, n))
    return out.reshape(depth * 6, 1, d)


_ROW_TM = 256


def _modulate_kernel(x_ref, shift_ref, scale_ref, h_ref):
    h_ref[...] = (x_ref[...] * (1.0 + scale_ref[...]) + shift_ref[...]).astype(h_ref.dtype)


def _modulate(x, mod, shift_row, scale_row):
    s, d = x.shape
    return pl.pallas_call(
        _modulate_kernel,
        out_shape=jax.ShapeDtypeStruct((s, d), _BF16),
        grid=(s // _ROW_TM,),
        in_specs=[
            pl.BlockSpec((_ROW_TM, d), lambda i: (i, 0)),
            pl.BlockSpec((None, 1, d), lambda i: (shift_row, 0, 0)),
            pl.BlockSpec((None, 1, d), lambda i: (scale_row, 0, 0)),
        ],
        out_specs=pl.BlockSpec((_ROW_TM, d), lambda i: (i, 0)),
        compiler_params=_params(("parallel",)),
        name="modulate",
    )(x, mod, mod)


_STREAM_ROWS = _ROW_TM // STREAMS


def _stream_permutation():
    assert _STREAM_ROWS == STREAMS
    dst = lax.broadcasted_iota(jnp.int32, (_ROW_TM, _ROW_TM), 0)
    src = lax.broadcasted_iota(jnp.int32, (_ROW_TM, _ROW_TM), 1)
    return (src == (dst % STREAMS) * STREAMS + dst // STREAMS).astype(_BF16)


def _ln_kernel(x_ref, y_ref, gate_ref, g_ref, b_ref, *rest, with_h, stream_h):
    if with_h:
        shift_ref, scale_ref, xo_ref, h_ref = rest
    else:
        (xo_ref,) = rest
    z = ALPHA * x_ref[...] + (1.0 + gate_ref[...]) * y_ref[...]
    mu = jnp.mean(z, axis=-1, keepdims=True)
    zc = z - mu
    var = jnp.mean(zc * zc, axis=-1, keepdims=True)
    xn = zc * lax.rsqrt(var + LN_EPS) * g_ref[...] + b_ref[...]
    xo_ref[...] = xn
    if with_h:
        h = (xn * (1.0 + scale_ref[...]) + shift_ref[...]).astype(h_ref.dtype)
        if stream_h:
            h = jnp.dot(_stream_permutation(), h, preferred_element_type=_F32).astype(h_ref.dtype)
            h = h.reshape(h_ref.shape)
        h_ref[...] = h


def _residual_ln(x, y, mod, gate_row, ln_g, ln_b, next_rows=None, stream_h=False):
    s, d = x.shape
    with_h = next_rows is not None
    row = lambda r: pl.BlockSpec((None, 1, d), lambda i: (r, 0, 0))
    vec = pl.BlockSpec((1, d), lambda i: (0, 0))
    tile = pl.BlockSpec((_ROW_TM, d), lambda i: (i, 0))
    in_specs = [tile, tile, row(gate_row), vec, vec]
    args = [x, y, mod, ln_g.reshape(1, d), ln_b.reshape(1, d)]
    out_shape = [jax.ShapeDtypeStruct((s, d), _F32)]
    out_specs = [tile]
    if with_h:
        in_specs += [row(next_rows[0]), row(next_rows[1])]
        args += [mod, mod]
        if stream_h:
            out_shape.append(jax.ShapeDtypeStruct((STREAMS, s // STREAMS, d), _BF16))
            out_specs.append(pl.BlockSpec((STREAMS, _STREAM_ROWS, d), lambda i: (0, i, 0)))
        else:
            out_shape.append(jax.ShapeDtypeStruct((s, d), _BF16))
            out_specs.append(tile)
    res = pl.pallas_call(
        functools.partial(_ln_kernel, with_h=with_h, stream_h=stream_h),
        out_shape=out_shape,
        grid=(s // _ROW_TM,),
        in_specs=in_specs,
        out_specs=out_specs,
        compiler_params=_params(("parallel",), 40 * 1024 * 1024),
        name="residual_ln",
    )(*args)
    return (res[0], res[1].reshape(s, d)) if with_h else (res[0], None)


_MM_TM = 1024
_MM_TN = 1024
_MM_TK = 4096
_MM_CHUNK = 256


def _rope_tile(y, c, sa, sb):
    return y * c + pltpu.roll(y, LANES - ROT_HALF, 1) * sa + pltpu.roll(y, ROT_HALF, 1) * sb


def _mm_qkv_kernel(a_ref, b_ref, c_ref, sa_ref, sb_ref, o_ref):
    tn = o_ref.shape[1]
    which = ((pl.program_id(1) * tn) // MIX_WIDTH) % 3
    is_v = which == 2
    qs = jnp.where(which == 0, Q_SCALE, 1.0).astype(_F32)
    c = jnp.where(is_v, 1.0, c_ref[...] * qs)
    sa = jnp.where(is_v, 0.0, sa_ref[...] * qs)
    sb = jnp.where(is_v, 0.0, sb_ref[...] * qs)
    for c0 in range(0, tn, _MM_CHUNK):
        acc = jnp.dot(a_ref[...], b_ref[:, c0:c0 + _MM_CHUNK], preferred_element_type=_F32)
        for h0 in range(0, _MM_CHUNK, HEAD_DIM):
            y = _rope_tile(acc[:, h0:h0 + HEAD_DIM], c, sa, sb)
            o_ref[:, c0 + h0:c0 + h0 + HEAD_DIM] = y.astype(o_ref.dtype)


def _matmul_qkv(a, b, rope_c, rope_sa, rope_sb):
    m, k = a.shape
    n = b.shape[1]
    tm, tn = min(_MM_TM, m), _MM_TN
    tab = pl.BlockSpec((tm, LANES), lambda i, j: (i, 0))
    return pl.pallas_call(
        _mm_qkv_kernel,
        out_shape=jax.ShapeDtypeStruct((m, n), _BF16),
        grid=(m // tm, n // tn),
        in_specs=[
            pl.BlockSpec((tm, k), lambda i, j: (i, 0)),
            pl.BlockSpec((k, tn), lambda i, j: (0, j)),
            tab, tab, tab,
        ],
        out_specs=pl.BlockSpec((tm, tn), lambda i, j: (i, j)),
        compiler_params=_params(("parallel", "arbitrary"), V7X_VMEM_LIMIT),
        name="matmul_qkv_rope",
    )(a, b, rope_c, rope_sa, rope_sb)


def _mm_kernel(*refs, act):
    *a_refs, b_ref, o_ref = refs
    acc, k0 = None, 0
    for a_ref in a_refs:
        k1 = k0 + a_ref.shape[1]
        part = jnp.dot(a_ref[...], b_ref[k0:k1, :], preferred_element_type=_F32)
        acc = part if acc is None else acc + part
        k0 = k1
    if act == "sqrelu":
        acc = jnp.square(jnp.maximum(acc, 0.0))
    o_ref[...] = acc.astype(o_ref.dtype)


def _mm_acc_kernel(a_ref, b_ref, o_ref):
    kk = pl.program_id(2)

    def chunks(first):
        for c0 in range(0, o_ref.shape[1], _MM_CHUNK):
            cols = slice(c0, c0 + _MM_CHUNK)
            part = jnp.dot(a_ref[...], b_ref[:, cols], preferred_element_type=_F32)
            o_ref[:, cols] = part if first else o_ref[:, cols] + part

    @pl.when(kk == 0)
    def _():
        chunks(True)

    @pl.when(kk > 0)
    def _():
        chunks(False)


def _matmul(a, b, out_dtype, act=None):
    parts = a if isinstance(a, tuple) else (a,)
    m = parts[0].shape[0]
    k, n = b.shape
    assert sum(p.shape[1] for p in parts) == k
    tm, tn = min(_MM_TM, m), _MM_TN
    if k <= 4096:
        return pl.pallas_call(
            functools.partial(_mm_kernel, act=act),
            out_shape=jax.ShapeDtypeStruct((m, n), out_dtype),
            grid=(m // tm, n // tn),
            in_specs=[pl.BlockSpec((tm, p.shape[1]), lambda i, j: (i, 0)) for p in parts]
            + [pl.BlockSpec((k, tn), lambda i, j: (0, j))],
            out_specs=pl.BlockSpec((tm, tn), lambda i, j: (i, j)),
            compiler_params=_params(("parallel", "arbitrary"), V7X_VMEM_LIMIT),
            name="matmul",
        )(*parts, b)
    assert act is None and len(parts) == 1 and out_dtype == _F32
    a = parts[0]
    tk = _MM_TK
    return pl.pallas_call(
        _mm_acc_kernel,
        out_shape=jax.ShapeDtypeStruct((m, n), out_dtype),
        grid=(m // tm, n // tn, k // tk),
        in_specs=[
            pl.BlockSpec((tm, tk), lambda i, j, kk: (i, kk)),
            pl.BlockSpec((tk, tn), lambda i, j, kk: (kk, j)),
        ],
        out_specs=pl.BlockSpec((tm, tn), lambda i, j, kk: (i, j)),
        compiler_params=_params(("parallel", "arbitrary", "arbitrary"), V7X_VMEM_LIMIT),
        name="matmul_kacc",
    )(a, b)


def _dot_nt(a, b):
    return lax.dot_general(a, b, (((1,), (1,)), ((), ())), preferred_element_type=_F32)


_MOBA_STRIP = 32
_MOBA_QBLOCKS = 4


def _moba_kernel(q_ref, k_ref, v_ref, o_ref, kmh_ref, kml_ref, s0_ref, s1_ref, p_ref, m_ref, alpha_ref,
                 acc_ref):
    i = pl.program_id(1)
    blk = MOBA_BLOCK
    pair = 2 * blk
    nb = k_ref.shape[0] // blk

    @pl.when(i == 0)
    def _():
        kmh_ref[...] = jnp.zeros_like(kmh_ref)
        kml_ref[...] = jnp.zeros_like(kml_ref)

        def body(b, carry):
            start = pl.multiple_of(b * blk, blk)
            kb = k_ref[pl.ds(start, blk), :].astype(_F32)
            mean = jnp.sum(kb, axis=0, keepdims=True) * (1.0 / blk)
            hi = mean.astype(_BF16).astype(_F32)
            kmh_ref[pl.ds(b, 1), :] = hi
            kml_ref[pl.ds(b, 1), :] = mean - hi
            return carry

        lax.fori_loop(0, nb, body, 0)

    qt = q_ref.shape[0]
    q = q_ref[...]
    gate_t = _dot_nt(kmh_ref[...].astype(_BF16), q) + _dot_nt(kml_ref[...].astype(_BF16), q)
    bidx = lax.broadcasted_iota(jnp.int32, (LANES, qt), 0)
    qblk = _MOBA_QBLOCKS * i + lax.broadcasted_iota(jnp.int32, (LANES, qt), 1) // blk
    g = jnp.where(bidx < qblk, gate_t, -jnp.inf)
    sel = jnp.zeros((LANES, qt), jnp.bool_)
    for r in range(MOBA_TOPK):
        mx = jnp.max(g, axis=0, keepdims=True)
        first = jnp.min(jnp.where(g == mx, bidx, LANES), axis=0, keepdims=True)
        pick = (bidx == first) & (r < qblk)
        sel = sel | pick
        g = jnp.where(pick, -jnp.inf, g)
    unsel = jnp.where(sel | (bidx >= qblk), 0.0, 1.0).astype(_F32).T
    q_aug = jnp.concatenate([q, unsel.astype(_BF16)], axis=1)

    lane = lax.broadcasted_iota(jnp.int32, (pair, LANES), 1)
    half = lax.broadcasted_iota(jnp.int32, (pair, LANES), 0) // blk
    ones = jnp.ones((pair, LANES), _BF16)

    def scores_to(slot, jp):
        start = pl.multiple_of(jp * pair, pair)
        w = jnp.where(lane == 2 * jp + half, NEG, 0.0).astype(_BF16)
        k_aug = jnp.concatenate([k_ref[pl.ds(start, pair), :], w], axis=1)
        (s0_ref, s1_ref)[slot][...] = _dot_nt(q_aug, k_aug)

    def update(slot, jp, key_offset=None):
        for r0 in range(0, qt, _MOBA_STRIP):
            rows = slice(r0, r0 + _MOBA_STRIP)
            s = (s0_ref, s1_ref)[slot][rows, :]
            if key_offset is not None:
                kpos = key_offset + lax.broadcasted_iota(jnp.int32, (_MOBA_STRIP, pair), 1)
                qpos = r0 + lax.broadcasted_iota(jnp.int32, (_MOBA_STRIP, pair), 0)
                s = jnp.where(kpos <= qpos, s, NEG)
            m_old = m_ref[rows, :]
            m_new = jnp.maximum(m_old, jnp.max(s, axis=-1, keepdims=True))
            m_ref[rows, :] = m_new
            alpha_ref[rows, :] = jnp.exp2(m_old - m_new)
            p_ref[rows, :] = jnp.exp2(s - jnp.concatenate([m_new] * (pair // LANES), axis=1)).astype(_BF16)
        start = pl.multiple_of(jp * pair, pair)
        v_aug = jnp.concatenate([v_ref[pl.ds(start, pair), :], ones], axis=1)
        alpha = alpha_ref[...]
        acc_ref[...] = (jnp.concatenate([alpha, alpha], axis=1) * acc_ref[...]
                        + jnp.dot(p_ref[...], v_aug, preferred_element_type=_F32))

    def body(u, carry):
        scores_to(1, 2 * u + 1)
        update(0, 2 * u)
        scores_to(0, 2 * u + 2)
        update(1, 2 * u + 1)
        return carry

    m_ref[...] = jnp.full(m_ref.shape, NEG, _F32)
    acc_ref[...] = jnp.zeros(acc_ref.shape, _F32)
    scores_to(0, 0)
    lax.fori_loop(0, i, body, 0)
    scores_to(1, 2 * i + 1)
    update(0, 2 * i, 0)
    update(1, 2 * i + 1, pair)
    o_ref[...] = (acc_ref[:, :HEAD_DIM] / acc_ref[:, HEAD_DIM:]).astype(o_ref.dtype)


def _moba_attention(qkv):
    s = qkv.shape[0]
    nb = s // MOBA_BLOCK
    assert nb % _MOBA_QBLOCKS == 0 and _MOBA_QBLOCKS == 4 and nb <= LANES
    nbp = LANES
    pair = 2 * MOBA_BLOCK
    qt = _MOBA_QBLOCKS * MOBA_BLOCK
    return pl.pallas_call(
        _moba_kernel,
        out_shape=jax.ShapeDtypeStruct((s, MIX_WIDTH), _BF16),
        grid=(N_HEADS, nb // _MOBA_QBLOCKS),
        in_specs=[
            pl.BlockSpec((qt, HEAD_DIM), lambda h, i: (i, h)),
            pl.BlockSpec((s, HEAD_DIM), lambda h, i: (0, N_HEADS + h)),
            pl.BlockSpec((s, HEAD_DIM), lambda h, i: (0, 2 * N_HEADS + h)),
        ],
        out_specs=pl.BlockSpec((qt, HEAD_DIM), lambda h, i: (i, h)),
        scratch_shapes=[pltpu.VMEM((nbp, HEAD_DIM), _F32), pltpu.VMEM((nbp, HEAD_DIM), _F32),
                        pltpu.VMEM((qt, pair), _F32),
                        pltpu.VMEM((qt, pair), _F32),
                        pltpu.VMEM((qt, pair), _BF16),
                        pltpu.VMEM((qt, LANES), _F32),
                        pltpu.VMEM((qt, LANES), _F32),
                        pltpu.VMEM((qt, 2 * HEAD_DIM), _F32)],
        compiler_params=_params(("parallel", "arbitrary"), 40 * 1024 * 1024),
        name="moba_attention",
    )(qkv, qkv, qkv)


_CONV_TR = 128
_CONV_HALO = 32
_CONV_LC = 128


def _conv_kernel(cur_ref, halo_ref, w_ref, cb_ref, g_ref, b_ref, o_ref, u_ref, y_ref, shift_ref):
    i = pl.program_id(0)
    ch = CONV_CH
    cur = cur_ref[...]
    u_ref[_CONV_HALO:, :] = cur[:, :ch] * jax.nn.sigmoid(cur[:, ch:])
    halo = halo_ref[...]
    uh = halo[:, :ch] * jax.nn.sigmoid(halo[:, ch:])
    u_ref[:_CONV_HALO, :] = jnp.where(i > 0, uh, 0.0)

    off = _CONV_HALO - (CONV_WIDTH - 1)
    for c0 in range(0, ch, _CONV_LC):
        cols = slice(c0, c0 + _CONV_LC)
        vshape = (_CONV_TR // 8, 8, _CONV_LC)
        acc = jnp.zeros(vshape, _F32) + jnp.broadcast_to(cb_ref[:, cols], vshape[1:])
        for b in range(8):
            a_max = (CONV_WIDTH - 1 - b) // 8
            span = 8 * a_max + _CONV_TR
            ub_ref = shift_ref.at[b % 2]
            ub_ref[0:span, :] = u_ref[off + b:off + b + span, cols]
            for a in range(a_max + 1):
                t = 8 * a + b
                wt = jnp.broadcast_to(w_ref[t:t + 1, cols], vshape[1:])
                acc = acc + ub_ref[8 * a:8 * a + _CONV_TR, :].reshape(vshape) * wt
        y_ref[:, cols] = acc.reshape(_CONV_TR, _CONV_LC)

    y = y_ref[...]
    mu = jnp.mean(y, axis=-1, keepdims=True)
    yc = y - mu
    var = jnp.mean(yc * yc, axis=-1, keepdims=True)
    yn = yc * lax.rsqrt(var + LN_EPS) * g_ref[...] + b_ref[...]
    o_ref[...] = (yn * jax.nn.sigmoid(yn)).astype(o_ref.dtype)


def _conv_module(glu, conv_w, conv_b, ln_g, ln_b):
    s = glu.shape[0]
    ch = CONV_CH
    ratio = _CONV_TR // _CONV_HALO
    vec = lambda: pl.BlockSpec((1, ch), lambda i: (0, 0))
    return pl.pallas_call(
        _conv_kernel,
        out_shape=jax.ShapeDtypeStruct((s, ch), _BF16),
        grid=(s // _CONV_TR,),
        in_specs=[
            pl.BlockSpec((_CONV_TR, 2 * ch), lambda i: (i, 0)),
            pl.BlockSpec((_CONV_HALO, 2 * ch), lambda i: (jnp.maximum(i * ratio - 1, 0), 0)),
            pl.BlockSpec((CONV_WIDTH, ch), lambda i: (0, 0)),
            vec(), vec(), vec(),
        ],
        out_specs=pl.BlockSpec((_CONV_TR, ch), lambda i: (i, 0)),
        scratch_shapes=[pltpu.VMEM((_CONV_TR + _CONV_HALO, ch), _F32), pltpu.VMEM((_CONV_TR, ch), _F32),
                        pltpu.VMEM((2, _CONV_TR + _CONV_HALO, _CONV_LC), _F32)],
        compiler_params=_params(("parallel",), 40 * 1024 * 1024),
        name="conformer_conv",
    )(glu, glu, conv_w, conv_b.reshape(1, ch), ln_g.reshape(1, ch), ln_b.reshape(1, ch))


def _dil_kernel(q_ref, kp_ref, kc_ref, vp_ref, vc_ref, o_ref, lse_ref, *, sub, chunk):
    n = pl.program_id(1)
    rows = sub * chunk

    def pos(axis):
        rho = lax.broadcasted_iota(jnp.int32, (rows, rows), axis)
        return sub * (rho % chunk) + rho // chunk

    dist = pos(0) - pos(1)
    mask_cur = (dist >= 0) & (dist <= DIL_WINDOW)
    mask_prev = (dist + rows <= DIL_WINDOW) & (n > 0)
    mask = jnp.concatenate([mask_prev, mask_cur], axis=1)
    lane = lax.broadcasted_iota(jnp.int32, (rows, LANES), 1)
    ones = jnp.ones((2 * rows, LANES), _BF16)
    lse_tile = jnp.zeros((rows, LANES), _F32)
    for hh in range(N_HEADS):
        sl = slice(hh * HEAD_DIM, (hh + 1) * HEAD_DIM)
        ld = lambda ref: ref[:, :, sl].reshape(rows, HEAD_DIM)
        k = jnp.concatenate([ld(kp_ref), ld(kc_ref)], axis=0)
        v = jnp.concatenate([jnp.concatenate([ld(vp_ref), ld(vc_ref)], axis=0), ones], axis=1)
        s = jnp.where(mask, _dot_nt(ld(q_ref), k), NEG)
        m = jnp.max(s, axis=-1, keepdims=True)
        ov = jnp.dot(jnp.exp2(s - m).astype(_BF16), v, preferred_element_type=_F32)
        l = ov[:, HEAD_DIM:]
        o_ref[:, :, sl] = (ov[:, :HEAD_DIM] / l).astype(o_ref.dtype).reshape(sub, chunk, HEAD_DIM)
        lse_tile = jnp.where(lane == hh, m + jnp.log2(l), lse_tile)
    lse_ref[...] = lse_tile.reshape(sub, chunk, LANES)


def _dilated_group(qkv, group, dil):
    s, c = qkv.shape
    sub = STREAMS // dil
    chunk = max(DIL_WINDOW // sub, 16)
    ls = s // STREAMS
    view = qkv.reshape(sub, dil, ls, c)
    base = 3 * group

    def spec(slot, prev, width=MIX_WIDTH):
        col = base + slot if width == MIX_WIDTH else 0
        if prev:
            return pl.BlockSpec((sub, None, chunk, width), lambda r, n: (0, r, jnp.maximum(n - 1, 0), col))
        return pl.BlockSpec((sub, None, chunk, width), lambda r, n: (0, r, n, col))

    o, lse = pl.pallas_call(
        functools.partial(_dil_kernel, sub=sub, chunk=chunk),
        out_shape=[jax.ShapeDtypeStruct((sub, dil, ls, MIX_WIDTH), _BF16),
                   jax.ShapeDtypeStruct((sub, dil, ls, LANES), _F32)],
        grid=(dil, ls // chunk),
        in_specs=[spec(0, False), spec(1, True), spec(1, False), spec(2, True), spec(2, False)],
        out_specs=[pl.BlockSpec((sub, None, chunk, MIX_WIDTH), lambda r, n: (0, r, n, 0)),
                   spec(0, False, LANES)],
        compiler_params=_params(("parallel", "arbitrary"), 40 * 1024 * 1024),
        name=f"dilated_attention_d{dil}",
    )(view, view, view, view, view)
    return o.reshape(s, MIX_WIDTH), lse.reshape(s, LANES)


def _merge_kernel(o0_ref, o1_ref, o2_ref, l0_ref, l1_ref, l2_ref, out_ref):
    flat = lambda ref: ref[...].reshape(_ROW_TM, ref.shape[-1])
    l0, l1, l2 = flat(l0_ref), flat(l1_ref), flat(l2_ref)
    mx = jnp.maximum(jnp.maximum(l0, l1), l2)
    e0, e1, e2 = jnp.exp2(l0 - mx), jnp.exp2(l1 - mx), jnp.exp2(l2 - mx)
    inv = 1.0 / (e0 + e1 + e2)
    w0, w1, w2 = e0 * inv, e1 * inv, e2 * inv
    perm = _stream_permutation()
    for hh in range(N_HEADS):
        sl = slice(hh * HEAD_DIM, (hh + 1) * HEAD_DIM)
        ld = lambda ref: ref[:, :, sl].reshape(_ROW_TM, HEAD_DIM).astype(_F32)
        o = w0[:, hh:hh + 1] * ld(o0_ref) + w1[:, hh:hh + 1] * ld(o1_ref) + w2[:, hh:hh + 1] * ld(o2_ref)
        out_ref[:, sl] = jnp.dot(perm, o.astype(_BF16), preferred_element_type=_F32).astype(out_ref.dtype)


def _merge_groups(outs, lses):
    s = outs[0].shape[0]
    view = lambda t: t.reshape(STREAMS, s // STREAMS, t.shape[-1])
    ot = pl.BlockSpec((STREAMS, _STREAM_ROWS, MIX_WIDTH), lambda i: (0, i, 0))
    lt = pl.BlockSpec((STREAMS, _STREAM_ROWS, LANES), lambda i: (0, i, 0))
    return pl.pallas_call(
        _merge_kernel,
        out_shape=jax.ShapeDtypeStruct((s, MIX_WIDTH), _BF16),
        grid=(s // _ROW_TM,),
        in_specs=[ot, ot, ot, lt, lt, lt],
        out_specs=pl.BlockSpec((_ROW_TM, MIX_WIDTH), lambda i: (i, 0)),
        compiler_params=_params(("parallel",), 40 * 1024 * 1024),
        name="merge_dilated_groups",
    )(*[view(t) for t in outs], *[view(t) for t in lses])


def _rope_tables(pos):
    s = pos.shape[0]
    pos = pos.astype(_F32)
    inv = ROPE_THETA ** (-jnp.arange(0, ROT_DIM, 2, dtype=_F32) / ROT_DIM)
    ang = pos[:, None] * inv[None, :]
    cos, sin = jnp.cos(ang), jnp.sin(ang)
    zeros = jnp.zeros((s, HEAD_DIM - ROT_DIM), _F32)
    zh = jnp.zeros((s, ROT_HALF), _F32)
    c = jnp.concatenate([cos, cos, jnp.ones((s, HEAD_DIM - ROT_DIM), _F32)], axis=1)
    sa = jnp.concatenate([-sin, zh, zeros], axis=1)
    sb = jnp.concatenate([zh, sin, zeros], axis=1)
    return c, sa, sb


def _forward(x, c, w_ada, b_ada, w_in_ab, w_out_ab, conv_w, conv_b, conv_ln_g, conv_ln_b,
             w_in_c, w_out_c, w_ff1, w_ff2, ln_g, ln_b):
    s, d = x.shape
    rope = _rope_tables(jnp.arange(s))
    rope_streams = _rope_tables(jnp.arange(s).reshape(s // STREAMS, STREAMS).T.reshape(s))
    mod = _ada_modulation(c.reshape(d), w_ada, b_ada)
    bf = lambda t: t.astype(_BF16)

    h = _modulate(x, mod, 0, 1)
    w_in = w_in_ab[0]
    qkv = _matmul_qkv(h, bf(w_in[:, :3 * MIX_WIDTH]), *rope)
    glu = _matmul(h, bf(w_in[:, 3 * MIX_WIDTH:]), _F32)
    a_out = _moba_attention(qkv)
    u = _conv_module(glu, conv_w[0], conv_b[0], conv_ln_g[0], conv_ln_b[0])
    y = _matmul((a_out, u), bf(w_out_ab[0]), _F32)
    x, h = _residual_ln(x, y, mod, 2, ln_g[0, 0], ln_b[0, 0], (3, 4))
    y = _matmul(_matmul(h, bf(w_ff1[0]), _BF16, act="sqrelu"), bf(w_ff2[0]), _F32)
    x, h = _residual_ln(x, y, mod, 5, ln_g[0, 1], ln_b[0, 1], (6, 7), stream_h=True)

    qkv = _matmul_qkv(h, bf(w_in_c[0]), *rope_streams)
    outs, lses = [], []
    for gi, (_, dil) in enumerate(DIL_CONFIGS):
        o, lse = _dilated_group(qkv, gi, dil)
        outs.append(o)
        lses.append(lse)
    y = _matmul(_merge_groups(outs, lses), bf(w_out_c[0]), _F32)
    x, h = _residual_ln(x, y, mod, 8, ln_g[1, 0], ln_b[1, 0], (9, 10))
    y = _matmul(_matmul(h, bf(w_ff1[1]), _BF16, act="sqrelu"), bf(w_ff2[1]), _F32)
    x, _ = _residual_ln(x, y, mod, 11, ln_g[1, 1], ln_b[1, 1])
    return x


def kernel(x, c, w_ada, b_ada, w_in_ab, w_out_ab, conv_w, conv_b, conv_ln_g, conv_ln_b,
           w_in_c, w_out_c, w_ff1, w_ff2, ln_g, ln_b):
    b, s, d = x.shape
    assert b == 1 and d == D_MODEL and s % (DIL_CONFIGS[-1][1] * DIL_WINDOW) == 0
    out = _forward(x.reshape(s, d), c, w_ada, b_ada, w_in_ab, w_out_ab, conv_w, conv_b,
                   conv_ln_g, conv_ln_b, w_in_c, w_out_c, w_ff1, w_ff2, ln_g, ln_b)
    return out.reshape(b, s, d)
```

```python
import functools

import jax
import jax.numpy as jnp
from jax import lax
from jax.experimental import pallas as pl
from jax.experimental.pallas import tpu as pltpu

D_MODEL = 4096
DEPTH = 2
HEAD_DIM = 128
ROT_DIM = HEAD_DIM // 4
ROT_HALF = ROT_DIM // 2
ROPE_THETA = 500000.0
N_HEADS = D_MODEL // (2 * HEAD_DIM)
MIX_WIDTH = N_HEADS * HEAD_DIM
MOBA_BLOCK = 256
MOBA_TOPK = 3
CONV_CH = D_MODEL // 2
CONV_WIDTH = 31
DIL_CONFIGS = ((128, 1), (512, 4), (2048, 16))
DIL_WINDOW = 128
D_FF = 4 * D_MODEL
ALPHA = (2 * DEPTH) ** 0.25
LN_EPS = 1e-5
SM_SCALE = HEAD_DIM ** -0.5
LOG2E = 1.4426950408889634
Q_SCALE = SM_SCALE * LOG2E
STREAMS = DIL_CONFIGS[-1][1]

LANES = 128
V7X_VMEM_LIMIT = 56 * 1024 * 1024
NEG = -1e30

_BF16 = jnp.bfloat16
_F32 = jnp.float32


def _params(sem, vmem=None):
    return pltpu.CompilerParams(dimension_semantics=sem, vmem_limit_bytes=vmem)


_ADA_TN = 512
_ADA_ROWS = 512


def _ada_kernel(c_ref, w_ref, b_ref, o_ref, cond_ref):
    @pl.when((pl.program_id(0) == 0) & (pl.program_id(1) == 0))
    def _():
        cv = c_ref[...]
        cond_ref[...] = cv * jax.nn.sigmoid(cv)

    d = w_ref.shape[0]
    acc = jnp.zeros((8, _ADA_TN), _F32)
    for r0 in range(0, d, _ADA_ROWS):
        cond = cond_ref[r0:r0 + _ADA_ROWS, :]
        w = w_ref[r0:r0 + _ADA_ROWS, :]
        cond_t = jnp.concatenate([cond] * (_ADA_TN // LANES), axis=1)
        prod = (w * cond_t).reshape(_ADA_ROWS // 8, 8, _ADA_TN)
        acc = acc + jnp.sum(prod, axis=0)
    o_ref[...] = jnp.sum(acc, axis=0, keepdims=True) + b_ref[...]


def _ada_modulation(c, w_ada, b_ada):
    depth, d, n = w_ada.shape
    c_rep = jnp.broadcast_to(c.reshape(d, 1), (d, LANES))
    out = pl.pallas_call(
        _ada_kernel,
        out_shape=jax.ShapeDtypeStruct((depth, 1, n), _F32),
        grid=(depth, n // _ADA_TN),
        in_specs=[
            pl.BlockSpec((d, LANES), lambda l, j: (0, 0)),
            pl.BlockSpec((None, d, _ADA_TN), lambda l, j: (l, 0, j)),
            pl.BlockSpec((None, 1, _ADA_TN), lambda l, j: (l, 0, j)),
        ],
        out_specs=pl.BlockSpec((None, 1, _ADA_TN), lambda l, j: (l, 0, j)),
        scratch_shapes=[pltpu.VMEM((d, LANES), _F32)],
        compiler_params=_params(("arbitrary", "arbitrary"), 40 * 1024 * 1024),
        name="ada_modulation",
    )(c_rep, w_ada, b_ada.reshape(depth, 1, n))
    return out.reshape(depth * 6, 1, d)


_ROW_TM = 256


def _modulate_kernel(x_ref, shift_ref, scale_ref, h_ref):
    h_ref[...] = (x_ref[...] * (1.0 + scale_ref[...]) + shift_ref[...]).astype(h_ref.dtype)


def _modulate(x, mod, shift_row, scale_row):
    s, d = x.shape
    return pl.pallas_call(
        _modulate_kernel,
        out_shape=jax.ShapeDtypeStruct((s, d), _BF16),
        grid=(s // _ROW_TM,),
        in_specs=[
            pl.BlockSpec((_ROW_TM, d), lambda i: (i, 0)),
            pl.BlockSpec((None, 1, d), lambda i: (shift_row, 0, 0)),
            pl.BlockSpec((None, 1, d), lambda i: (scale_row, 0, 0)),
        ],
        out_specs=pl.BlockSpec((_ROW_TM, d), lambda i: (i, 0)),
        compiler_params=_params(("parallel",)),
        name="modulate",
    )(x, mod, mod)


_STREAM_ROWS = _ROW_TM // STREAMS


def _stream_permutation():
    assert _STREAM_ROWS == STREAMS
    dst = lax.broadcasted_iota(jnp.int32, (_ROW_TM, _ROW_TM), 0)
    src = lax.broadcasted_iota(jnp.int32, (_ROW_TM, _ROW_TM), 1)
    return (src == (dst % STREAMS) * STREAMS + dst // STREAMS).astype(_BF16)


def _ln_kernel(x_ref, y_ref, gate_ref, g_ref, b_ref, *rest, with_h, stream_h):
    if with_h:
        shift_ref, scale_ref, xo_ref, h_ref = rest
    else:
        (xo_ref,) = rest
    z = ALPHA * x_ref[...] + (1.0 + gate_ref[...]) * y_ref[...]
    mu = jnp.mean(z, axis=-1, keepdims=True)
    zc = z - mu
    var = jnp.mean(zc * zc, axis=-1, keepdims=True)
    xn = zc * lax.rsqrt(var + LN_EPS) * g_ref[...] + b_ref[...]
    xo_ref[...] = xn
    if with_h:
        h = (xn * (1.0 + scale_ref[...]) + shift_ref[...]).astype(h_ref.dtype)
        if stream_h:
            h = jnp.dot(_stream_permutation(), h, preferred_element_type=_F32).astype(h_ref.dtype)
            h = h.reshape(h_ref.shape)
        h_ref[...] = h


def _residual_ln(x, y, mod, gate_row, ln_g, ln_b, next_rows=None, stream_h=False):
    s, d = x.shape
    with_h = next_rows is not None
    row = lambda r: pl.BlockSpec((None, 1, d), lambda i: (r, 0, 0))
    vec = pl.BlockSpec((1, d), lambda i: (0, 0))
    tile = pl.BlockSpec((_ROW_TM, d), lambda i: (i, 0))
    in_specs = [tile, tile, row(gate_row), vec, vec]
    args = [x, y, mod, ln_g.reshape(1, d), ln_b.reshape(1, d)]
    out_shape = [jax.ShapeDtypeStruct((s, d), _F32)]
    out_specs = [tile]
    if with_h:
        in_specs += [row(next_rows[0]), row(next_rows[1])]
        args += [mod, mod]
        if stream_h:
            out_shape.append(jax.ShapeDtypeStruct((STREAMS, s // STREAMS, d), _BF16))
            out_specs.append(pl.BlockSpec((STREAMS, _STREAM_ROWS, d), lambda i: (0, i, 0)))
        else:
            out_shape.append(jax.ShapeDtypeStruct((s, d), _BF16))
            out_specs.append(tile)
    res = pl.pallas_call(
        functools.partial(_ln_kernel, with_h=with_h, stream_h=stream_h),
        out_shape=out_shape,
        grid=(s // _ROW_TM,),
        in_specs=in_specs,
        out_specs=out_specs,
        compiler_params=_params(("parallel",), 40 * 1024 * 1024),
        name="residual_ln",
    )(*args)
    return (res[0], res[1].reshape(s, d)) if with_h else (res[0], None)


_MM_TM = 1024
_MM_TN = 1024
_MM_TK = 4096
_MM_CHUNK = 256


def _rope_tile(y, c, sa, sb):
    return y * c + pltpu.roll(y, LANES - ROT_HALF, 1) * sa + pltpu.roll(y, ROT_HALF, 1) * sb


def _mm_qkv_kernel(a_ref, b_ref, c_ref, sa_ref, sb_ref, o_ref):
    tn = o_ref.shape[1]
    which = ((pl.program_id(1) * tn) // MIX_WIDTH) % 3
    @pl.when(which == 2)
    def _():
        for c0 in range(0, tn, _MM_CHUNK):
            cols = slice(c0, c0 + _MM_CHUNK)
            o_ref[:, cols] = jnp.dot(a_ref[...], b_ref[:, cols], preferred_element_type=_F32).astype(o_ref.dtype)

    @pl.when(which != 2)
    def _():
        qs = jnp.where(which == 0, Q_SCALE, 1.0).astype(_F32)
        c, sa, sb = c_ref[...] * qs, sa_ref[...] * qs, sb_ref[...] * qs
        for c0 in range(0, tn, _MM_CHUNK):
            acc = jnp.dot(a_ref[...], b_ref[:, c0:c0 + _MM_CHUNK], preferred_element_type=_F32)
            for h0 in range(0, _MM_CHUNK, HEAD_DIM):
                y = _rope_tile(acc[:, h0:h0 + HEAD_DIM], c, sa, sb)
                o_ref[:, c0 + h0:c0 + h0 + HEAD_DIM] = y.astype(o_ref.dtype)


def _matmul_qkv(a, b, rope_c, rope_sa, rope_sb):
    m, k = a.shape
    n = b.shape[1]
    tm, tn = min(_MM_TM, m), _MM_TN
    tab = pl.BlockSpec((tm, LANES), lambda i, j: (i, 0))
    return pl.pallas_call(
        _mm_qkv_kernel,
        out_shape=jax.ShapeDtypeStruct((m, n), _BF16),
        grid=(m // tm, n // tn),
        in_specs=[
            pl.BlockSpec((tm, k), lambda i, j: (i, 0)),
            pl.BlockSpec((k, tn), lambda i, j: (0, j)),
            tab, tab, tab,
        ],
        out_specs=pl.BlockSpec((tm, tn), lambda i, j: (i, j)),
        compiler_params=_params(("parallel", "arbitrary"), V7X_VMEM_LIMIT),
        name="matmul_qkv_rope",
    )(a, b, rope_c, rope_sa, rope_sb)


def _mm_kernel(*refs, act):
    *a_refs, b_ref, o_ref = refs
    acc, k0 = None, 0
    for a_ref in a_refs:
        k1 = k0 + a_ref.shape[1]
        part = jnp.dot(a_ref[...], b_ref[k0:k1, :], preferred_element_type=_F32)
        acc = part if acc is None else acc + part
        k0 = k1
    if act == "sqrelu":
        acc = jnp.square(jnp.maximum(acc, 0.0))
    o_ref[...] = acc.astype(o_ref.dtype)


def _mm_acc_kernel(a_ref, b_ref, o_ref):
    kk = pl.program_id(2)

    def chunks(first):
        for c0 in range(0, o_ref.shape[1], _MM_CHUNK):
            cols = slice(c0, c0 + _MM_CHUNK)
            part = jnp.dot(a_ref[...], b_ref[:, cols], preferred_element_type=_F32)
            o_ref[:, cols] = part if first else o_ref[:, cols] + part

    @pl.when(kk == 0)
    def _():
        chunks(True)

    @pl.when(kk > 0)
    def _():
        chunks(False)


def _matmul(a, b, out_dtype, act=None):
    parts = a if isinstance(a, tuple) else (a,)
    m = parts[0].shape[0]
    k, n = b.shape
    assert sum(p.shape[1] for p in parts) == k
    tm, tn = min(_MM_TM, m), _MM_TN
    if k <= 4096:
        return pl.pallas_call(
            functools.partial(_mm_kernel, act=act),
            out_shape=jax.ShapeDtypeStruct((m, n), out_dtype),
            grid=(m // tm, n // tn),
            in_specs=[pl.BlockSpec((tm, p.shape[1]), lambda i, j: (i, 0)) for p in parts]
            + [pl.BlockSpec((k, tn), lambda i, j: (0, j))],
            out_specs=pl.BlockSpec((tm, tn), lambda i, j: (i, j)),
            compiler_params=_params(("parallel", "arbitrary"), V7X_VMEM_LIMIT),
            name="matmul",
        )(*parts, b)
    assert act is None and len(parts) == 1 and out_dtype == _F32
    a = parts[0]
    tk = _MM_TK
    return pl.pallas_call(
        _mm_acc_kernel,
        out_shape=jax.ShapeDtypeStruct((m, n), out_dtype),
        grid=(m // tm, n // tn, k // tk),
        in_specs=[
            pl.BlockSpec((tm, tk), lambda i, j, kk: (i, kk)),
            pl.BlockSpec((tk, tn), lambda i, j, kk: (kk, j)),
        ],
        out_specs=pl.BlockSpec((tm, tn), lambda i, j, kk: (i, j)),
        compiler_params=_params(("parallel", "arbitrary", "arbitrary"), V7X_VMEM_LIMIT),
        name="matmul_kacc",
    )(a, b)


def _dot_nt(a, b):
    return lax.dot_general(a, b, (((1,), (1,)), ((), ())), preferred_element_type=_F32)


_MOBA_STRIP = 32
_MOBA_QBLOCKS = 4


def _moba_kernel(q_ref, k_ref, v_ref, o_ref, kmh_ref, kml_ref, s0_ref, s1_ref, p_ref, m_ref, alpha_ref,
                 acc_ref):
    i = pl.program_id(1)
    blk = MOBA_BLOCK
    pair = 2 * blk
    nb = k_ref.shape[0] // blk

    @pl.when(i == 0)
    def _():
        kmh_ref[...] = jnp.zeros_like(kmh_ref)
        kml_ref[...] = jnp.zeros_like(kml_ref)

        def body(b, carry):
            start = pl.multiple_of(b * blk, blk)
            kb = k_ref[pl.ds(start, blk), :].astype(_F32)
            mean = jnp.sum(kb, axis=0, keepdims=True) * (1.0 / blk)
            hi = mean.astype(_BF16).astype(_F32)
            kmh_ref[pl.ds(b, 1), :] = hi
            kml_ref[pl.ds(b, 1), :] = mean - hi
            return carry

        lax.fori_loop(0, nb, body, 0)

    qt = q_ref.shape[0]
    q = q_ref[...]
    gate_t = _dot_nt(kmh_ref[...].astype(_BF16), q) + _dot_nt(kml_ref[...].astype(_BF16), q)
    bidx = lax.broadcasted_iota(jnp.int32, (LANES, qt), 0)
    qblk = _MOBA_QBLOCKS * i + lax.broadcasted_iota(jnp.int32, (LANES, qt), 1) // blk
    g = jnp.where(bidx < qblk, gate_t, -jnp.inf)
    sel = jnp.zeros((LANES, qt), jnp.bool_)
    for r in range(MOBA_TOPK):
        mx = jnp.max(g, axis=0, keepdims=True)
        first = jnp.min(jnp.where(g == mx, bidx, LANES), axis=0, keepdims=True)
        pick = (bidx == first) & (r < qblk)
        sel = sel | pick
        g = jnp.where(pick, -jnp.inf, g)
    unsel = jnp.where(sel | (bidx >= qblk), 0.0, 1.0).astype(_F32).T
    q_aug = jnp.concatenate([q, unsel.astype(_BF16)], axis=1)

    lane = lax.broadcasted_iota(jnp.int32, (pair, LANES), 1)
    half = lax.broadcasted_iota(jnp.int32, (pair, LANES), 0) // blk
    ones = jnp.ones((pair, LANES), _BF16)

    def scores_to(slot, jp):
        start = pl.multiple_of(jp * pair, pair)
        w = jnp.where(lane == 2 * jp + half, NEG, 0.0).astype(_BF16)
        k_aug = jnp.concatenate([k_ref[pl.ds(start, pair), :], w], axis=1)
        (s0_ref, s1_ref)[slot][...] = _dot_nt(q_aug, k_aug)

    def update(slot, jp, key_offset=None):
        first = 0 if key_offset is None else key_offset
        live = slice(first, qt)
        for r0 in range(first, qt, _MOBA_STRIP):
            rows = slice(r0, r0 + _MOBA_STRIP)
            s = (s0_ref, s1_ref)[slot][rows, :]
            if key_offset is not None and r0 < key_offset + pair:
                kpos = key_offset + lax.broadcasted_iota(jnp.int32, (_MOBA_STRIP, pair), 1)
                qpos = r0 + lax.broadcasted_iota(jnp.int32, (_MOBA_STRIP, pair), 0)
                s = jnp.where(kpos <= qpos, s, NEG)
            m_old = m_ref[rows, :]
            m_new = jnp.maximum(m_old, jnp.max(s, axis=-1, keepdims=True))
            m_ref[rows, :] = m_new
            alpha_ref[rows, :] = jnp.exp2(m_old - m_new)
            p_ref[rows, :] = jnp.exp2(s - jnp.concatenate([m_new] * (pair // LANES), axis=1)).astype(_BF16)
        start = pl.multiple_of(jp * pair, pair)
        v_aug = jnp.concatenate([v_ref[pl.ds(start, pair), :], ones], axis=1)
        alpha = alpha_ref[live, :]
        acc_ref[live, :] = (jnp.concatenate([alpha, alpha], axis=1) * acc_ref[live, :]
                            + jnp.dot(p_ref[live, :], v_aug, preferred_element_type=_F32))

    def body(u, carry):
        scores_to(1, 2 * u + 1)
        update(0, 2 * u)
        scores_to(0, 2 * u + 2)
        update(1, 2 * u + 1)
        return carry

    m_ref[...] = jnp.full(m_ref.shape, NEG, _F32)
    acc_ref[...] = jnp.zeros(acc_ref.shape, _F32)
    scores_to(0, 0)
    lax.fori_loop(0, i, body, 0)
    scores_to(1, 2 * i + 1)
    update(0, 2 * i, 0)
    update(1, 2 * i + 1, pair)
    o_ref[...] = (acc_ref[:, :HEAD_DIM] / acc_ref[:, HEAD_DIM:]).astype(o_ref.dtype)


def _moba_attention(qkv):
    s = qkv.shape[0]
    nb = s // MOBA_BLOCK
    assert nb % _MOBA_QBLOCKS == 0 and _MOBA_QBLOCKS == 4 and nb <= LANES
    nbp = LANES
    pair = 2 * MOBA_BLOCK
    qt = _MOBA_QBLOCKS * MOBA_BLOCK
    return pl.pallas_call(
        _moba_kernel,
        out_shape=jax.ShapeDtypeStruct((s, MIX_WIDTH), _BF16),
        grid=(N_HEADS, nb // _MOBA_QBLOCKS),
        in_specs=[
            pl.BlockSpec((qt, HEAD_DIM), lambda h, i: (i, h)),
            pl.BlockSpec((s, HEAD_DIM), lambda h, i: (0, N_HEADS + h)),
            pl.BlockSpec((s, HEAD_DIM), lambda h, i: (0, 2 * N_HEADS + h)),
        ],
        out_specs=pl.BlockSpec((qt, HEAD_DIM), lambda h, i: (i, h)),
        scratch_shapes=[pltpu.VMEM((nbp, HEAD_DIM), _F32), pltpu.VMEM((nbp, HEAD_DIM), _F32),
                        pltpu.VMEM((qt, pair), _F32),
                        pltpu.VMEM((qt, pair), _F32),
                        pltpu.VMEM((qt, pair), _BF16),
                        pltpu.VMEM((qt, LANES), _F32),
                        pltpu.VMEM((qt, LANES), _F32),
                        pltpu.VMEM((qt, 2 * HEAD_DIM), _F32)],
        compiler_params=_params(("parallel", "arbitrary"), 40 * 1024 * 1024),
        name="moba_attention",
    )(qkv, qkv, qkv)


_CONV_TR = 128
_CONV_HALO = 32
_CONV_LC = 128


def _conv_kernel(cur_ref, halo_ref, w_ref, cb_ref, g_ref, b_ref, o_ref, u_ref, y_ref, shift_ref):
    i = pl.program_id(0)
    ch = CONV_CH
    cur = cur_ref[...]
    u_ref[_CONV_HALO:, :] = cur[:, :ch] * jax.nn.sigmoid(cur[:, ch:])
    halo = halo_ref[...]
    uh = halo[:, :ch] * jax.nn.sigmoid(halo[:, ch:])
    u_ref[:_CONV_HALO, :] = jnp.where(i > 0, uh, 0.0)

    off = _CONV_HALO - (CONV_WIDTH - 1)
    for c0 in range(0, ch, _CONV_LC):
        cols = slice(c0, c0 + _CONV_LC)
        vshape = (_CONV_TR // 8, 8, _CONV_LC)
        acc = jnp.zeros(vshape, _F32) + jnp.broadcast_to(cb_ref[:, cols], vshape[1:])
        for b in range(8):
            a_max = (CONV_WIDTH - 1 - b) // 8
            span = 8 * a_max + _CONV_TR
            ub_ref = shift_ref.at[b % 2]
            ub_ref[0:span, :] = u_ref[off + b:off + b + span, cols]
            for a in range(a_max + 1):
                t = 8 * a + b
                wt = jnp.broadcast_to(w_ref[t:t + 1, cols], vshape[1:])
                acc = acc + ub_ref[8 * a:8 * a + _CONV_TR, :].reshape(vshape) * wt
        y_ref[:, cols] = acc.reshape(_CONV_TR, _CONV_LC)

    y = y_ref[...]
    mu = jnp.mean(y, axis=-1, keepdims=True)
    yc = y - mu
    var = jnp.mean(yc * yc, axis=-1, keepdims=True)
    yn = yc * lax.rsqrt(var + LN_EPS) * g_ref[...] + b_ref[...]
    o_ref[...] = (yn * jax.nn.sigmoid(yn)).astype(o_ref.dtype)


def _conv_module(glu, conv_w, conv_b, ln_g, ln_b):
    s = glu.shape[0]
    ch = CONV_CH
    ratio = _CONV_TR // _CONV_HALO
    vec = lambda: pl.BlockSpec((1, ch), lambda i: (0, 0))
    return pl.pallas_call(
        _conv_kernel,
        out_shape=jax.ShapeDtypeStruct((s, ch), _BF16),
        grid=(s // _CONV_TR,),
        in_specs=[
            pl.BlockSpec((_CONV_TR, 2 * ch), lambda i: (i, 0)),
            pl.BlockSpec((_CONV_HALO, 2 * ch), lambda i: (jnp.maximum(i * ratio - 1, 0), 0)),
            pl.BlockSpec((CONV_WIDTH, ch), lambda i: (0, 0)),
            vec(), vec(), vec(),
        ],
        out_specs=pl.BlockSpec((_CONV_TR, ch), lambda i: (i, 0)),
        scratch_shapes=[pltpu.VMEM((_CONV_TR + _CONV_HALO, ch), _F32), pltpu.VMEM((_CONV_TR, ch), _F32),
                        pltpu.VMEM((2, _CONV_TR + _CONV_HALO, _CONV_LC), _F32)],
        compiler_params=_params(("parallel",), 40 * 1024 * 1024),
        name="conformer_conv",
    )(glu, glu, conv_w, conv_b.reshape(1, ch), ln_g.reshape(1, ch), ln_b.reshape(1, ch))


def _dil_kernel(q_ref, kp_ref, kc_ref, vp_ref, vc_ref, o_ref, lse_ref, *, sub, chunk):
    n = pl.program_id(1)
    rows = sub * chunk

    def pos(axis):
        rho = lax.broadcasted_iota(jnp.int32, (rows, rows), axis)
        return sub * (rho % chunk) + rho // chunk

    dist = pos(0) - pos(1)
    mask_cur = (dist >= 0) & (dist <= DIL_WINDOW)
    mask_prev = (dist + rows <= DIL_WINDOW) & (n > 0)
    mask = jnp.concatenate([mask_prev, mask_cur], axis=1)
    lane = lax.broadcasted_iota(jnp.int32, (rows, LANES), 1)
    ones = jnp.ones((2 * rows, LANES), _BF16)
    lse_tile = jnp.zeros((rows, LANES), _F32)
    for hh in range(N_HEADS):
        sl = slice(hh * HEAD_DIM, (hh + 1) * HEAD_DIM)
        ld = lambda ref: ref[:, :, sl].reshape(rows, HEAD_DIM)
        k = jnp.concatenate([ld(kp_ref), ld(kc_ref)], axis=0)
        v = jnp.concatenate([jnp.concatenate([ld(vp_ref), ld(vc_ref)], axis=0), ones], axis=1)
        s = jnp.where(mask, _dot_nt(ld(q_ref), k), NEG)
        m = jnp.max(s, axis=-1, keepdims=True)
        ov = jnp.dot(jnp.exp2(s - m).astype(_BF16), v, preferred_element_type=_F32)
        l = ov[:, HEAD_DIM:]
        o_ref[:, :, sl] = (ov[:, :HEAD_DIM] / l).astype(o_ref.dtype).reshape(sub, chunk, HEAD_DIM)
        lse_tile = jnp.where(lane == hh, m + jnp.log2(l), lse_tile)
    lse_ref[...] = lse_tile.reshape(sub, chunk, LANES)


def _dilated_group(qkv, group, dil):
    s, c = qkv.shape
    sub = STREAMS // dil
    chunk = max(DIL_WINDOW // sub, 16)
    ls = s // STREAMS
    view = qkv.reshape(sub, dil, ls, c)
    base = 3 * group

    def spec(slot, prev, width=MIX_WIDTH):
        col = base + slot if width == MIX_WIDTH else 0
        if prev:
            return pl.BlockSpec((sub, None, chunk, width), lambda r, n: (0, r, jnp.maximum(n - 1, 0), col))
        return pl.BlockSpec((sub, None, chunk, width), lambda r, n: (0, r, n, col))

    o, lse = pl.pallas_call(
        functools.partial(_dil_kernel, sub=sub, chunk=chunk),
        out_shape=[jax.ShapeDtypeStruct((sub, dil, ls, MIX_WIDTH), _BF16),
                   jax.ShapeDtypeStruct((sub, dil, ls, LANES), _F32)],
        grid=(dil, ls // chunk),
        in_specs=[spec(0, False), spec(1, True), spec(1, False), spec(2, True), spec(2, False)],
        out_specs=[pl.BlockSpec((sub, None, chunk, MIX_WIDTH), lambda r, n: (0, r, n, 0)),
                   spec(0, False, LANES)],
        compiler_params=_params(("parallel", "arbitrary"), 40 * 1024 * 1024),
        name=f"dilated_attention_d{dil}",
    )(view, view, view, view, view)
    return o.reshape(s, MIX_WIDTH), lse.reshape(s, LANES)


def _merge_kernel(o0_ref, o1_ref, o2_ref, l0_ref, l1_ref, l2_ref, out_ref):
    flat = lambda ref: ref[...].reshape(_ROW_TM, ref.shape[-1])
    l0, l1, l2 = flat(l0_ref), flat(l1_ref), flat(l2_ref)
    mx = jnp.maximum(jnp.maximum(l0, l1), l2)
    e0, e1, e2 = jnp.exp2(l0 - mx), jnp.exp2(l1 - mx), jnp.exp2(l2 - mx)
    inv = 1.0 / (e0 + e1 + e2)
    w0, w1, w2 = e0 * inv, e1 * inv, e2 * inv
    perm = _stream_permutation()
    for hh in range(N_HEADS):
        sl = slice(hh * HEAD_DIM, (hh + 1) * HEAD_DIM)
        ld = lambda ref: ref[:, :, sl].reshape(_ROW_TM, HEAD_DIM).astype(_F32)
        o = w0[:, hh:hh + 1] * ld(o0_ref) + w1[:, hh:hh + 1] * ld(o1_ref) + w2[:, hh:hh + 1] * ld(o2_ref)
        out_ref[:, sl] = jnp.dot(perm, o.astype(_BF16), preferred_element_type=_F32).astype(out_ref.dtype)


def _merge_groups(outs, lses):
    s = outs[0].shape[0]
    view = lambda t: t.reshape(STREAMS, s // STREAMS, t.shape[-1])
    ot = pl.BlockSpec((STREAMS, _STREAM_ROWS, MIX_WIDTH), lambda i: (0, i, 0))
    lt = pl.BlockSpec((STREAMS, _STREAM_ROWS, LANES), lambda i: (0, i, 0))
    return pl.pallas_call(
        _merge_kernel,
        out_shape=jax.ShapeDtypeStruct((s, MIX_WIDTH), _BF16),
        grid=(s // _ROW_TM,),
        in_specs=[ot, ot, ot, lt, lt, lt],
        out_specs=pl.BlockSpec((_ROW_TM, MIX_WIDTH), lambda i: (i, 0)),
        compiler_params=_params(("parallel",), 40 * 1024 * 1024),
        name="merge_dilated_groups",
    )(*[view(t) for t in outs], *[view(t) for t in lses])


def _rope_tables(pos):
    s = pos.shape[0]
    pos = pos.astype(_F32)
    inv = ROPE_THETA ** (-jnp.arange(0, ROT_DIM, 2, dtype=_F32) / ROT_DIM)
    ang = pos[:, None] * inv[None, :]
    cos, sin = jnp.cos(ang), jnp.sin(ang)
    zeros = jnp.zeros((s, HEAD_DIM - ROT_DIM), _F32)
    zh = jnp.zeros((s, ROT_HALF), _F32)
    c = jnp.concatenate([cos, cos, jnp.ones((s, HEAD_DIM - ROT_DIM), _F32)], axis=1)
    sa = jnp.concatenate([-sin, zh, zeros], axis=1)
    sb = jnp.concatenate([zh, sin, zeros], axis=1)
    return c, sa, sb


def _forward(x, c, w_ada, b_ada, w_in_ab, w_out_ab, conv_w, conv_b, conv_ln_g, conv_ln_b,
             w_in_c, w_out_c, w_ff1, w_ff2, ln_g, ln_b):
    s, d = x.shape
    rope = _rope_tables(jnp.arange(s))
    rope_streams = _rope_tables(jnp.arange(s).reshape(s // STREAMS, STREAMS).T.reshape(s))
    mod = _ada_modulation(c.reshape(d), w_ada, b_ada)
    bf = lambda t: t.astype(_BF16)

    h = _modulate(x, mod, 0, 1)
    w_in = w_in_ab[0]
    qkv = _matmul_qkv(h, bf(w_in[:, :3 * MIX_WIDTH]), *rope)
    glu = _matmul(h, bf(w_in[:, 3 * MIX_WIDTH:]), _F32)
    a_out = _moba_attention(qkv)
    u = _conv_module(glu, conv_w[0], conv_b[0], conv_ln_g[0], conv_ln_b[0])
    y = _matmul((a_out, u), bf(w_out_ab[0]), _F32)
    x, h = _residual_ln(x, y, mod, 2, ln_g[0, 0], ln_b[0, 0], (3, 4))
    y = _matmul(_matmul(h, bf(w_ff1[0]), _BF16, act="sqrelu"), bf(w_ff2[0]), _F32)
    x, h = _residual_ln(x, y, mod, 5, ln_g[0, 1], ln_b[0, 1], (6, 7), stream_h=True)

    qkv = _matmul_qkv(h, bf(w_in_c[0]), *rope_streams)
    outs, lses = [], []
    for gi, (_, dil) in enumerate(DIL_CONFIGS):
        o, lse = _dilated_group(qkv, gi, dil)
        outs.append(o)
        lses.append(lse)
    y = _matmul(_merge_groups(outs, lses), bf(w_out_c[0]), _F32)
    x, h = _residual_ln(x, y, mod, 8, ln_g[1, 0], ln_b[1, 0], (9, 10))
    y = _matmul(_matmul(h, bf(w_ff1[1]), _BF16, act="sqrelu"), bf(w_ff2[1]), _F32)
    x, _ = _residual_ln(x, y, mod, 11, ln_g[1, 1], ln_b[1, 1])
    return x


def kernel(x, c, w_ada, b_ada, w_in_ab, w_out_ab, conv_w, conv_b, conv_ln_g, conv_ln_b,
           w_in_c, w_out_c, w_ff1, w_ff2, ln_g, ln_b):
    b, s, d = x.shape
    assert b == 1 and d == D_MODEL and s % (DIL_CONFIGS[-1][1] * DIL_WINDOW) == 0
    out = _forward(x.reshape(s, d), c, w_ada, b_ada, w_in_ab, w_out_ab, conv_w, conv_b,
                   conv_ln_g, conv_ln_b, w_in_c, w_out_c, w_ff1, w_ff2, ln_g, ln_b)
    return out.reshape(b, s, d)
```
